```python
import jax, jax.numpy as jnp
from jax import lax
import numpy as np

D_MODEL = 1024
BATCH = 32
SEQ = 2048
DEPTH = 1

CTX_LEN = 256
GRID_W = 64
N_MOD = 6

POOL_WINDOWS = (2, 4, 8, 16)
N_POOL_GROUPS = len(POOL_WINDOWS)
POOL_GROUP_DIM = D_MODEL // 8
D_POOL = N_POOL_GROUPS * POOL_GROUP_DIM

HEAD_DIM = 64
N_HEADS = D_MODEL // HEAD_DIM
D_RWKV = N_HEADS * HEAD_DIM
N_DIR = 2
LORA_W = 64
LORA_A = 64
LORA_G = 128
SHORT_CONV = 3
D_CONV = 3 * D_RWKV + N_DIR * LORA_W + N_DIR * LORA_A + LORA_G
N_BRANCHES = 2
D_IN = D_POOL + D_CONV + N_BRANCHES * D_MODEL
GN_EPS = 64e-5
RMS_EPS = 1e-6
DECAY_OFFSET = 0.5

N_GROUPS = 4
EXPERTS_PER_GROUP = 8
N_EXPERTS = N_GROUPS * EXPERTS_PER_GROUP
TOP_K = 2
D_EXPERT = D_MODEL // 4

kernel_name = 'hybrid_pool_rwkv7_hmoe_dit'


def rmsnorm(x, g):
    xf = x.astype(jnp.float32)
    return xf * lax.rsqrt(jnp.mean(xf * xf, axis=-1, keepdims=True) + RMS_EPS) * g


def modulate(h, shift, scale):
    return h * (1.0 + scale) + shift


def short_conv(u, w):
    pad = SHORT_CONV // 2
    return lax.conv_general_dilated(
        u, w[:, None, :].astype(u.dtype), window_strides=(1,), padding=[(pad, pad)],
        dimension_numbers=('NWC', 'WIO', 'NWC'), feature_group_count=u.shape[-1])


def box_mean(u, axis, win):
    n = u.shape[axis]
    lo = win // 2
    hi = win - lo - 1
    cs = jnp.cumsum(u, axis=axis)
    cs = jnp.concatenate([jnp.zeros_like(lax.slice_in_dim(cs, 0, 1, axis=axis)), cs], axis=axis)
    t = jnp.arange(n)
    upper = jnp.minimum(t + hi + 1, n)
    lower = jnp.maximum(t - lo, 0)
    total = jnp.take(cs, upper, axis=axis) - jnp.take(cs, lower, axis=axis)
    shape = [1] * u.ndim
    shape[axis] = n
    return total / (upper - lower).astype(u.dtype).reshape(shape)


def pool_mix(u, pool_w, pool_scale, rows):
    b, t, _ = u.shape
    uf = u.astype(jnp.float32).reshape(b, t, N_POOL_GROUPS, POOL_GROUP_DIM)
    diffs = []
    for gi, win in enumerate(POOL_WINDOWS):
        ug = uf[:, :, gi]
        if rows is None:
            m = box_mean(ug, 1, win)
        else:
            grid = ug.reshape(b, rows, GRID_W, POOL_GROUP_DIM)
            m = box_mean(box_mean(grid, 1, win), 2, win).reshape(b, t, POOL_GROUP_DIM)
        diffs.append(m - ug)
    d = jnp.stack(diffs, axis=2)
    y = jnp.einsum('btgc,gcd->btgd', d, pool_w)
    return y.reshape(b, t, D_POOL) * pool_scale


def rwkv7_scan(s0, r, w, k, v, kk, bb, reverse):
    xs = tuple(jnp.moveaxis(a, 1, 0) for a in (r, w, k, v, kk, bb))

    def step(s, inp):
        r_t, w_t, k_t, v_t, kk_t, b_t = inp
        sa = jnp.einsum('bhvk,bhk->bhv', s, -kk_t)
        s = s * w_t[:, :, None, :] + sa[..., None] * b_t[:, :, None, :] + v_t[..., None] * k_t[:, :, None, :]
        return s, jnp.einsum('bhvk,bhk->bhv', s, r_t)

    s_fin, ys = lax.scan(step, s0, xs, reverse=reverse)
    return jnp.moveaxis(ys, 0, 1), s_fin


def group_norm(y, wgt, bias):
    mu = jnp.mean(y, axis=-1, keepdims=True)
    var = jnp.mean(jnp.square(y - mu), axis=-1, keepdims=True)
    yn = (y - mu) * lax.rsqrt(var + GN_EPS)
    b, t = y.shape[:2]
    return yn.reshape(b, t, D_RWKV) * wgt + bias


def rwkv_time_mix(rw, init_states, w0, w2, a0, a2, g2, k_k, k_a, r_k, lnx_w, lnx_b):
    b, t, _ = rw.shape
    rw = rw.astype(jnp.float32)

    def heads(a):
        return a.reshape(b, t, N_HEADS, HEAD_DIM)

    r = rw[..., :D_RWKV]
    k = rw[..., D_RWKV:2 * D_RWKV]
    v = rw[..., 2 * D_RWKV:3 * D_RWKV]
    o = 3 * D_RWKV
    xw = rw[..., o:o + N_DIR * LORA_W].reshape(b, t, N_DIR, LORA_W)
    o += N_DIR * LORA_W
    xa = rw[..., o:o + N_DIR * LORA_A].reshape(b, t, N_DIR, LORA_A)
    o += N_DIR * LORA_A
    xg = rw[..., o:o + LORA_G]
    g = jax.nn.sigmoid(xg) @ g2
    kk = heads(k * k_k)
    kk = kk / jnp.maximum(jnp.linalg.norm(kk, axis=-1, keepdims=True), 1e-12)
    r_h, v_h = heads(r), heads(v)
    outs, finals = [], []
    for d in range(N_DIR):
        w_log = -jax.nn.softplus(-(w0[d] + jnp.tanh(xw[:, :, d]) @ w2[d])) - DECAY_OFFSET
        decay = jnp.exp(-jnp.exp(w_log))
        a = jax.nn.sigmoid(a0[d] + xa[:, :, d] @ a2[d])
        k_d = heads(k * (1.0 + (a - 1.0) * k_a))
        y, s_fin = rwkv7_scan(init_states[d], r_h, heads(decay), k_d, v_h, kk, kk * heads(a), reverse=(d == 1))
        bonus = (jnp.sum(r_h * k_d * r_k, axis=-1, keepdims=True) * v_h).reshape(b, t, D_RWKV)
        outs.append(group_norm(y, lnx_w, lnx_b) + bonus)
        finals.append(s_fin)
    return (outs[0] + outs[1]) * g, finals


def merge_branches(pool_y, rwkv_y, gate_logits, pool_proj, rwkv_proj, w_out):
    g_pool = jax.nn.sigmoid(gate_logits[..., :D_MODEL])
    g_rwkv = jax.nn.sigmoid(gate_logits[..., D_MODEL:])
    return (g_pool * (pool_y @ pool_proj) + g_rwkv * (rwkv_y @ rwkv_proj)) @ w_out


def hier_moe(u, rg_w, rg_b, re_w, re_b, w_gate, w_up, w_down):
    b, t, d = u.shape
    flat = u.reshape(b * t, d)
    n = flat.shape[0]
    group_p = jax.nn.softmax((flat @ rg_w + rg_b).astype(jnp.float32), axis=-1)
    g_idx = jnp.argmax(group_p, axis=-1)
    p_group = jnp.max(group_p, axis=-1, keepdims=True)
    e_logits = (flat @ re_w + re_b).astype(jnp.float32).reshape(n, N_GROUPS, EXPERTS_PER_GROUP)
    e_p = jax.nn.softmax(e_logits[jnp.arange(n), g_idx], axis=-1)
    top_p, top_i = lax.top_k(e_p, TOP_K)
    top_p = top_p / jnp.sum(top_p, axis=-1, keepdims=True)
    within = jnp.einsum('nk,nke->ne', top_p, jax.nn.one_hot(top_i, EXPERTS_PER_GROUP, dtype=jnp.float32))
    gates = (jax.nn.one_hot(g_idx, N_GROUPS, dtype=jnp.float32)[:, :, None]
             * (p_group * within)[:, None, :]).reshape(n, N_EXPERTS)
    y = jnp.zeros((n, d), jnp.float32)
    for e in range(N_EXPERTS):
        h = jax.nn.silu(flat @ w_gate[e]) * (flat @ w_up[e])
        y = y + gates[:, e:e + 1] * (h @ w_down[e])
    return y.reshape(b, t, d)


def setup_inputs(seed: int = 0) -> dict:
    key = jax.random.key(seed)
    ks = iter(jax.random.split(key, 40))
    L = DEPTH

    def nrm(shape, scale):
        return jax.random.normal(next(ks), shape, jnp.float32) * scale

    conv_base = jnp.array([0.25, 0.5, 0.25], jnp.float32)[None, :, None]
    return {
        'x': nrm((BATCH, SEQ, D_MODEL), 1.0),
        'c': nrm((BATCH, D_MODEL), 1.0),
        'ctx': nrm((BATCH, CTX_LEN, D_MODEL), 1.0),
        'c_ctx': nrm((D_MODEL,), 1.0),
        'ada_w': nrm((L, D_MODEL, N_MOD * D_MODEL), 0.5 * D_MODEL ** -0.5),
        'ada_b': nrm((L, N_MOD * D_MODEL), 0.02),
        'norm_gains': 1.0 + nrm((L, 4, D_MODEL), 0.05),
        'w_in': nrm((L, D_MODEL, D_IN), D_MODEL ** -0.5),
        'conv_w': conv_base + nrm((L, SHORT_CONV, D_CONV), 0.05),
        'pool_w': nrm((L, N_POOL_GROUPS, POOL_GROUP_DIM, POOL_GROUP_DIM), POOL_GROUP_DIM ** -0.5),
        'pool_scale': 1.0 + nrm((L, D_POOL), 0.05),
        'pool_proj': nrm((L, D_POOL, D_MODEL), D_POOL ** -0.5),
        'w0': jax.random.uniform(next(ks), (L, N_DIR, D_RWKV), jnp.float32, -6.0, 0.0),
        'w2': nrm((L, N_DIR, LORA_W, D_RWKV), 0.1),
        'a0': nrm((L, N_DIR, D_RWKV), 0.5),
        'a2': nrm((L, N_DIR, LORA_A, D_RWKV), 0.1),
        'g2': nrm((L, LORA_G, D_RWKV), LORA_G ** -0.5),
        'k_k': 0.85 + nrm((L, D_RWKV), 0.05),
        'k_a': 1.0 + nrm((L, D_RWKV), 0.05),
        'r_k': nrm((L, N_HEADS, HEAD_DIM), 0.1),
        'lnx_w': 1.0 + nrm((L, D_RWKV), 0.05),
        'lnx_b': nrm((L, D_RWKV), 0.02),
        'rwkv_proj': nrm((L, D_RWKV, D_MODEL), D_RWKV ** -0.5),
        'w_out': nrm((L, D_MODEL, D_MODEL), D_MODEL ** -0.5),
        'router_group_w': nrm((L, D_MODEL, N_GROUPS), D_MODEL ** -0.5),
        'router_group_b': nrm((L, N_GROUPS), 0.01),
        'router_expert_w': nrm((L, D_MODEL, N_EXPERTS), D_MODEL ** -0.5),
        'router_expert_b': nrm((L, N_EXPERTS), 0.01),
        'expert_w_gate': nrm((L, N_EXPERTS, D_MODEL, D_EXPERT), D_MODEL ** -0.5),
        'expert_w_up': nrm((L, N_EXPERTS, D_MODEL, D_EXPERT), D_MODEL ** -0.5),
        'expert_w_down': nrm((L, N_EXPERTS, D_EXPERT, D_MODEL), D_EXPERT ** -0.5),
    }


def reference(x, c, ctx, c_ctx, ada_w, ada_b, norm_gains, w_in, conv_w, pool_w, pool_scale,
              pool_proj, w0, w2, a0, a2, g2, k_k, k_a, r_k, lnx_w, lnx_b, rwkv_proj, w_out,
              router_group_w, router_group_b, router_expert_w, router_expert_b,
              expert_w_gate, expert_w_up, expert_w_down):
    b = x.shape[0]
    rows = x.shape[1] // GRID_W
    zero_state = jnp.zeros((b, N_HEADS, HEAD_DIM, HEAD_DIM), jnp.float32)
    h_lat = x.astype(jnp.float32)
    h_ctx = ctx.astype(jnp.float32)
    for l in range(DEPTH):
        last = l == DEPTH - 1
        sh1, sc1, gt1, sh2, sc2, gt2 = jnp.split(
            (jax.nn.silu(c) @ ada_w[l] + ada_b[l])[:, None, :], N_MOD, axis=-1)
        csh1, csc1, cgt1, csh2, csc2, cgt2 = jnp.split(
            (jax.nn.silu(c_ctx) @ ada_w[l] + ada_b[l])[None, None, :], N_MOD, axis=-1)
        rw_params = (w0[l], w2[l], a0[l], a2[l], g2[l], k_k[l], k_a[l], r_k[l], lnx_w[l], lnx_b[l])
        moe_params = (router_group_w[l], router_group_b[l], router_expert_w[l], router_expert_b[l],
                      expert_w_gate[l], expert_w_up[l], expert_w_down[l])

        p_ctx = modulate(rmsnorm(h_ctx, norm_gains[l, 0]), csh1, csc1) @ w_in[l]
        p_lat = modulate(rmsnorm(h_lat, norm_gains[l, 0]), sh1, sc1) @ w_in[l]
        y_ctx_rwkv, ctx_states = rwkv_time_mix(
            short_conv(p_ctx[..., D_POOL:D_POOL + D_CONV], conv_w[l]), (zero_state, zero_state), *rw_params)
        y_lat_rwkv, _ = rwkv_time_mix(
            short_conv(p_lat[..., D_POOL:D_POOL + D_CONV], conv_w[l]), ctx_states, *rw_params)
        m_lat = merge_branches(pool_mix(p_lat[..., :D_POOL], pool_w[l], pool_scale[l], rows), y_lat_rwkv,
                               p_lat[..., D_POOL + D_CONV:], pool_proj[l], rwkv_proj[l], w_out[l])
        h_lat = h_lat + gt1 * rmsnorm(m_lat, norm_gains[l, 1])

        f_lat = hier_moe(modulate(rmsnorm(h_lat, norm_gains[l, 2]), sh2, sc2), *moe_params)
        h_lat = h_lat + gt2 * rmsnorm(f_lat, norm_gains[l, 3])

        if not last:
            m_ctx = merge_branches(pool_mix(p_ctx[..., :D_POOL], pool_w[l], pool_scale[l], None), y_ctx_rwkv,
                                   p_ctx[..., D_POOL + D_CONV:], pool_proj[l], rwkv_proj[l], w_out[l])
            h_ctx = h_ctx + cgt1 * rmsnorm(m_ctx, norm_gains[l, 1])
            f_ctx = hier_moe(modulate(rmsnorm(h_ctx, norm_gains[l, 2]), csh2, csc2), *moe_params)
            h_ctx = h_ctx + cgt2 * rmsnorm(f_ctx, norm_gains[l, 3])
    return h_lat.astype(x.dtype)
```

```python
import functools
import math

import jax
import jax.numpy as jnp
from jax import lax
from jax.experimental import pallas as pl
from jax.experimental.pallas import tpu as pltpu

F32 = jnp.float32
BF16 = jnp.bfloat16

GRID_W = 64
POOL_WINDOWS = (2, 4, 8, 16)
POOL_GROUP_DIM = 128
HEAD_DIM = 64
LORA_W = 64
LORA_A = 64
LORA_G = 128
N_GROUPS = 4
EXPERTS_PER_GROUP = 8
GN_EPS = 64e-5
RMS_EPS = 1e-6
DECAY_OFFSET = 0.5

CHUNK = 64
HEADS_PER_LANE_GROUP = 4
LANE_GROUP = HEADS_PER_LANE_GROUP * HEAD_DIM
VMEM_LIMIT = 56 * 1024 * 1024


def _bdot(a, b):
    return jnp.dot(a.astype(BF16), b.astype(BF16), preferred_element_type=F32)


def _bdot_t(a, b):
    return lax.dot_general(a.astype(BF16), b.astype(BF16), (((1,), (1,)), ((), ())),
                           preferred_element_type=F32)


def _split3(a):
    hi = a.astype(BF16)
    r1 = a - hi.astype(F32)
    mid = r1.astype(BF16)
    lo = (r1 - mid.astype(F32)).astype(BF16)
    return hi, mid, lo


def _dot_f32(a, b):
    a0, a1, a2 = _split3(a)
    b0, b1, b2 = _split3(b)
    d = lambda p, q: jnp.dot(p, q, preferred_element_type=F32)
    return (d(a0, b0) + (d(a0, b1) + d(a1, b0))
            + (d(a0, b2) + d(a1, b1) + d(a2, b0)))


def _sigmoid(x):
    return 1.0 / (1.0 + jnp.exp(-x))


def _softplus(x):
    return jnp.maximum(x, 0.0) + jnp.log(1.0 + jnp.exp(-jnp.abs(x)))


def _rms_scale(x):
    return lax.rsqrt(jnp.mean(x * x, axis=-1, keepdims=True) + RMS_EPS)


def _ada_kernel(c_ref, w_ref, b_ref, o_ref):
    c = c_ref[...]
    o_ref[...] = _dot_f32(c * _sigmoid(c), w_ref[...]) + b_ref[...]


def _ada(cc, ada_w, ada_b):
    rows, d = cc.shape
    n = ada_w.shape[1]
    tn = 1024
    return pl.pallas_call(
        _ada_kernel,
        out_shape=jax.ShapeDtypeStruct((rows, n), F32),
        grid=(n // tn,),
        in_specs=[pl.BlockSpec((rows, d), lambda j: (0, 0)),
                  pl.BlockSpec((d, tn), lambda j: (0, j)),
                  pl.BlockSpec((1, tn), lambda j: (0, j))],
        out_specs=pl.BlockSpec((rows, tn), lambda j: (0, j)),
        compiler_params=pltpu.CompilerParams(dimension_semantics=("arbitrary",),
                                             vmem_limit_bytes=VMEM_LIMIT),
        name="ada",
    )(cc, ada_w, ada_b)


def _inproj_conv_kernel(x_ref, mod_ref, w_ref, cw_ref, o_ref, hn_ref, *, n_ctx, row_blk):
    seq = x_ref.shape[1]

    @pl.when(pl.program_id(1) == 0)
    def _():
        gain = mod_ref[0, 4:5, :]
        for r0 in range(0, seq, row_blk):
            is_ctx = r0 < n_ctx
            sh = mod_ref[0, 2:3, :] if is_ctx else mod_ref[0, 0:1, :]
            sc = mod_ref[0, 3:4, :] if is_ctx else mod_ref[0, 1:2, :]
            x = x_ref[0, r0:r0 + row_blk, :]
            hn = x * _rms_scale(x) * gain * (1.0 + sc) + sh
            hn_ref[r0:r0 + row_blk, :] = hn.astype(BF16)

    p = jnp.dot(hn_ref[...], w_ref[...], preferred_element_type=F32)
    row = lax.broadcasted_iota(jnp.int32, p.shape, 0)
    prev = pltpu.roll(p, 1, 0)
    prev = jnp.where((row == 0) | (row == n_ctx), 0.0, prev)
    nxt = pltpu.roll(p, seq - 1, 0)
    nxt = jnp.where((row == n_ctx - 1) | (row == seq - 1), 0.0, nxt)
    out = cw_ref[0:1, :] * prev + cw_ref[1:2, :] * p + cw_ref[2:3, :] * nxt
    o_ref[0] = out.astype(BF16)


def _inproj_conv(xcat, mod, w_conv, conv_w, n_ctx):
    b, seq, d = xcat.shape
    n = w_conv.shape[1]
    tn = 384
    row_blk = math.gcd(math.gcd(seq, n_ctx), 256)
    assert n % tn == 0 and seq % row_blk == 0 and n_ctx % row_blk == 0
    return pl.pallas_call(
        functools.partial(_inproj_conv_kernel, n_ctx=n_ctx, row_blk=row_blk),
        out_shape=jax.ShapeDtypeStruct((b, seq, n), BF16),
        grid=(b, n // tn),
        in_specs=[pl.BlockSpec((1, seq, d), lambda i, j: (i, 0, 0)),
                  pl.BlockSpec((1, 8, d), lambda i, j: (i, 0, 0)),
                  pl.BlockSpec((d, tn), lambda i, j: (0, j)),
                  pl.BlockSpec((8, tn), lambda i, j: (0, j))],
        out_specs=pl.BlockSpec((1, seq, tn), lambda i, j: (i, 0, j)),
        scratch_shapes=[pltpu.VMEM((seq, d), BF16)],
        compiler_params=pltpu.CompilerParams(dimension_semantics=("parallel", "arbitrary"),
                                             vmem_limit_bytes=VMEM_LIMIT),
        name="inproj_conv",
    )(xcat, mod, w_conv, conv_w)


def _inproj_pg_kernel(x_ref, mod_ref, wp_ref, wg_ref, pool_ref, gate_ref):
    x = x_ref[0]
    hn = (x * _rms_scale(x) * mod_ref[0, 4:5, :] * (1.0 + mod_ref[0, 1:2, :]) + mod_ref[0, 0:1, :]).astype(BF16)
    pool_ref[0] = jnp.dot(hn, wp_ref[...], preferred_element_type=F32).astype(BF16)
    gate_ref[0] = _sigmoid(jnp.dot(hn, wg_ref[...], preferred_element_type=F32)).astype(BF16)


def _inproj_pg(x, mod, w_pool, w_gate):
    b, t, d = x.shape
    tm = min(512, t)
    assert t % tm == 0
    npool, ngate = w_pool.shape[1], w_gate.shape[1]
    return pl.pallas_call(
        _inproj_pg_kernel,
        out_shape=(jax.ShapeDtypeStruct((b, t, npool), BF16), jax.ShapeDtypeStruct((b, t, ngate), BF16)),
        grid=(b, t // tm),
        in_specs=[pl.BlockSpec((1, tm, d), lambda i, j: (i, j, 0)),
                  pl.BlockSpec((1, 8, d), lambda i, j: (i, 0, 0)),
                  pl.BlockSpec((d, npool), lambda i, j: (0, 0)),
                  pl.BlockSpec((d, ngate), lambda i, j: (0, 0))],
        out_specs=(pl.BlockSpec((1, tm, npool), lambda i, j: (i, j, 0)),
                   pl.BlockSpec((1, tm, ngate), lambda i, j: (i, j, 0))),
        compiler_params=pltpu.CompilerParams(dimension_semantics=("parallel", "parallel"),
                                             vmem_limit_bytes=VMEM_LIMIT),
        name="inproj_pool_gate",
    )(x, mod, w_pool, w_gate)


def _stack(x, bd):
    xb = x.astype(BF16)
    return jnp.concatenate([xb] * HEADS_PER_LANE_GROUP, axis=0) * bd


def _rwkv_kernel(xf_ref, xb_ref, w2_ref, a2_ref, vec_ref, bd_ref, icat_ref, mask_ref, tri_ref,
                 of_ref, ob_ref, s_ref, *, d_rwkv):
    step = pl.program_id(1)

    @pl.when(step == 0)
    def _():
        s_ref[...] = jnp.zeros_like(s_ref)

    bd = bd_ref[...]
    icat = icat_ref[...]
    n_lane_groups = d_rwkv // LANE_GROUP
    o_xw = 3 * d_rwkv
    o_xa = o_xw + 2 * LORA_W

    def bmm(a, b):
        return jnp.dot(a.astype(BF16), _stack(b, bd), preferred_element_type=F32)

    for d in range(2):
        x_ref = (xf_ref, xb_ref)[d]
        o_ref = (of_ref, ob_ref)[d]
        strict = mask_ref[d, 0]
        incl = mask_ref[d, 1]
        tri = tri_ref[d]
        tanh_xw = jnp.tanh(x_ref[0, :, o_xw:o_xw + 2 * LORA_W].astype(F32)).astype(BF16)
        xa = x_ref[0, :, o_xa:o_xa + 2 * LORA_A]
        last = CHUNK - 1 if d == 0 else 0
        for g in range(n_lane_groups):
            lo = g * LANE_GROUP
            hi = lo + LANE_GROUP
            r = x_ref[0, :, lo:hi].astype(F32)
            k = x_ref[0, :, d_rwkv + lo:d_rwkv + hi].astype(F32)
            v = x_ref[0, :, 2 * d_rwkv + lo:2 * d_rwkv + hi].astype(F32)
            w0 = vec_ref[d:d + 1, lo:hi]
            a0 = vec_ref[2 + d:3 + d, lo:hi]
            k_k = vec_ref[4:5, lo:hi]
            k_a = vec_ref[5:6, lo:hi]
            r_k = vec_ref[6:7, lo:hi]
            lnw = vec_ref[7:8, lo:hi]
            lnb = vec_ref[8:9, lo:hi]

            z = w0 + jnp.dot(tanh_xw, w2_ref[d, :, lo:hi], preferred_element_type=F32)
            lw = -jnp.exp(-_softplus(-z) - DECAY_OFFSET)
            a = _sigmoid(a0 + jnp.dot(xa, a2_ref[d, :, lo:hi], preferred_element_type=F32))
            kk = k * k_k
            kk = kk / jnp.maximum(jnp.sqrt(_bdot(kk * kk, bd)), 1e-12)
            kd = k * (1.0 + (a - 1.0) * k_a)

            l0, l1, l2 = _split3(lw)
            cl = (jnp.dot(tri, l0, preferred_element_type=F32)
                  + jnp.dot(tri, l1, preferred_element_type=F32)
                  + jnp.dot(tri, l2, preferred_element_type=F32))
            e_pos = jnp.exp(cl)
            e_neg = jnp.exp(-cl)
            at = -kk * jnp.exp(cl - lw)
            rt = r * e_pos
            bt = kk * a * e_neg
            kt = kd * e_neg

            lhs = jnp.concatenate([at, rt, icat], axis=0)
            gb = _bdot_t(lhs, _stack(bt, bd))
            gk = _bdot_t(lhs, _stack(kt, bd))
            a_ab = gb[0:CHUNK] * strict
            a_rb = gb[CHUNK:2 * CHUNK] * incl
            bt_t = gb[2 * CHUNK:3 * CHUNK]
            a_ak = gk[0:CHUNK] * strict
            a_rk = gk[CHUNK:2 * CHUNK] * incl
            kt_t = gk[2 * CHUNK:3 * CHUNK]

            tinv = icat + a_ab
            lp = a_ab
            for _ in range(5):
                lp = bmm(lp, lp)
                tinv = tinv + bmm(tinv, lp)

            hv = jnp.dot(jnp.concatenate([a_ak, a_rk, kt_t], axis=0).astype(BF16), _stack(v, bd),
                         preferred_element_type=F32)
            akv = hv[0:CHUNK]
            arkv = hv[CHUNK:2 * CHUNK]
            ktv = hv[2 * CHUNK:3 * CHUNK]
            wt = bmm(tinv, at)
            ut = bmm(tinv, akv)
            lhs2 = jnp.concatenate([a_rb, bt_t], axis=0).astype(BF16)
            hw = jnp.dot(lhs2, _stack(wt, bd), preferred_element_type=F32)
            hu = jnp.dot(lhs2, _stack(ut, bd), preferred_element_type=F32)
            q = rt + hw[0:CHUNK]
            zmat = hw[CHUNK:2 * CHUNK]
            y_intra = hu[0:CHUNK] + arkv
            g_new = hu[CHUNK:2 * CHUNK] + ktv

            gam = e_pos[last:last + 1, :]
            gam_hi = gam.astype(BF16)
            gam_lo = (gam - gam_hi.astype(F32)).astype(BF16)
            icat_b = icat.astype(BF16)
            gcat = (jnp.dot(icat_b * gam_hi, bd, preferred_element_type=F32)
                    + jnp.dot(icat_b * gam_lo, bd, preferred_element_type=F32))

            s0 = s_ref[d, g]
            hs = jnp.dot(jnp.concatenate([q, zmat], axis=0).astype(BF16), _stack(s0, bd),
                         preferred_element_type=F32)
            y = hs[0:CHUNK] + y_intra
            s_ref[d, g] = gcat * (s0 + hs[CHUNK:2 * CHUNK] + g_new)

            inv_n = 1.0 / HEAD_DIM
            mu = _bdot(y, bd) * inv_n
            yc = y - mu
            var = _bdot(yc * yc, bd) * inv_n
            bonus = _bdot(r * kd * r_k, bd) * v
            out = yc * lax.rsqrt(var + GN_EPS) * lnw + lnb + bonus
            o_ref[0, :, lo:hi] = out.astype(BF16)


def _rwkv(pconv, w2p, a2p, vecs, n_ctx, d_rwkv):
    b, seq, n = pconv.shape
    nc = seq // CHUNK
    nc_ctx = n_ctx // CHUNK
    assert seq % CHUNK == 0 and n_ctx % CHUNK == 0 and d_rwkv % LANE_GROUP == 0
    n_lane_groups = d_rwkv // LANE_GROUP

    lane = jnp.arange(LANE_GROUP)
    rowi = jnp.arange(CHUNK)
    j = (lane % HEAD_DIM)[None, :]
    t = rowi[:, None]
    bd = (lane[:, None] // HEAD_DIM == lane[None, :] // HEAD_DIM).astype(BF16)
    icat = (j == t).astype(F32)
    masks = jnp.stack([jnp.stack([(j < t), (j <= t)]), jnp.stack([(j > t), (j >= t)])]).astype(F32)
    tri = jnp.stack([rowi[None, :] <= rowi[:, None], rowi[None, :] >= rowi[:, None]]).astype(BF16)

    def bwd_chunk(s):
        return jnp.where(s < nc_ctx, nc_ctx - 1 - s, nc + nc_ctx - 1 - s)

    const2 = lambda i, s: (0, 0)
    const3 = lambda i, s: (0, 0, 0)
    const4 = lambda i, s: (0, 0, 0, 0)
    return pl.pallas_call(
        functools.partial(_rwkv_kernel, d_rwkv=d_rwkv),
        out_shape=(jax.ShapeDtypeStruct((b, seq, d_rwkv), BF16), jax.ShapeDtypeStruct((b, seq, d_rwkv), BF16)),
        grid=(b, nc),
        in_specs=[pl.BlockSpec((1, CHUNK, n), lambda i, s: (i, s, 0)),
                  pl.BlockSpec((1, CHUNK, n), lambda i, s: (i, bwd_chunk(s), 0)),
                  pl.BlockSpec(w2p.shape, const3),
                  pl.BlockSpec(a2p.shape, const3),
                  pl.BlockSpec(vecs.shape, const2),
                  pl.BlockSpec(bd.shape, const2),
                  pl.BlockSpec(icat.shape, const2),
                  pl.BlockSpec(masks.shape, const4),
                  pl.BlockSpec(tri.shape, const3)],
        out_specs=(pl.BlockSpec((1, CHUNK, d_rwkv), lambda i, s: (i, s, 0)),
                   pl.BlockSpec((1, CHUNK, d_rwkv), lambda i, s: (i, bwd_chunk(s), 0))),
        scratch_shapes=[pltpu.VMEM((2, n_lane_groups, CHUNK, LANE_GROUP), F32)],
        compiler_params=pltpu.CompilerParams(dimension_semantics=("parallel", "arbitrary"),
                                             vmem_limit_bytes=VMEM_LIMIT),
        name="rwkv_scan",
    )(pconv, pconv, w2p, a2p, vecs, bd, icat, masks, tri)


def _pool_kernel(u_ref, win_ref, inv_ref, w_ref, sc_ref, o_ref):
    u = u_ref[0]
    mean = jnp.dot(win_ref[0], u, preferred_element_type=F32) * inv_ref[0]
    diff = mean - u.astype(F32)
    o_ref[0] = (_bdot(diff, w_ref[0]) * sc_ref[0]).astype(BF16)


def _pool(pool_in, pool_w, pool_scale):
    b, t, dp = pool_in.shape
    ng = len(POOL_WINDOWS)
    cg = POOL_GROUP_DIM
    assert dp == ng * cg and t % GRID_W == 0
    tok = jnp.arange(t)
    row, col = tok // GRID_W, tok % GRID_W
    wins, invs = [], []
    for win in POOL_WINDOWS:
        lo = win // 2
        hi = win - lo - 1
        dr = row[None, :] - row[:, None]
        dc = col[None, :] - col[:, None]
        m = (dr >= -lo) & (dr <= hi) & (dc >= -lo) & (dc <= hi)
        wins.append(m.astype(BF16))
        invs.append(1.0 / jnp.sum(m, axis=1, dtype=F32))
    wins = jnp.stack(wins)
    invs = jnp.broadcast_to(jnp.stack(invs)[:, :, None], (ng, t, cg))
    return pl.pallas_call(
        _pool_kernel,
        out_shape=jax.ShapeDtypeStruct((b, t, dp), BF16),
        grid=(ng, b),
        in_specs=[pl.BlockSpec((1, t, cg), lambda g, i: (i, 0, g)),
                  pl.BlockSpec((1, t, t), lambda g, i: (g, 0, 0)),
                  pl.BlockSpec((1, t, cg), lambda g, i: (g, 0, 0)),
                  pl.BlockSpec((1, cg, cg), lambda g, i: (g, 0, 0)),
                  pl.BlockSpec((1, 1, cg), lambda g, i: (g, 0, 0))],
        out_specs=pl.BlockSpec((1, t, cg), lambda g, i: (i, 0, g)),
        compiler_params=pltpu.CompilerParams(dimension_semantics=("arbitrary", "arbitrary"),
                                             vmem_limit_bytes=VMEM_LIMIT),
        name="pool_mix",
    )(pool_in, wins, invs, pool_w.astype(BF16), pool_scale.reshape(ng, 1, cg))


def _merge_kernel(x_ref, py_ref, yf_ref, yb_ref, xg_ref, gate_ref, mod_ref, g2_ref, pp_ref, rp_ref, wo_ref, o_ref):
    d = x_ref.shape[2]
    g = jnp.dot(_sigmoid(xg_ref[0].astype(F32)).astype(BF16), g2_ref[...], preferred_element_type=F32)
    ry = (yf_ref[0].astype(F32) + yb_ref[0].astype(F32)) * g
    gates = gate_ref[0]
    m = (gates[:, :d].astype(F32) * jnp.dot(py_ref[0], pp_ref[...], preferred_element_type=F32)
         + gates[:, d:].astype(F32) * _bdot(ry, rp_ref[...]))
    mo = _bdot(m, wo_ref[...])
    o_ref[0] = x_ref[0] + mod_ref[0, 5:6, :] * (mo * _rms_scale(mo) * mod_ref[0, 6:7, :])


def _merge(x, pool_y, yf, yb, pconv, gates, mod, g2, pool_proj, rwkv_proj, w_out, n_ctx):
    b, t, d = x.shape
    tm = math.gcd(math.gcd(t, n_ctx), 256)
    off = n_ctx // tm
    dr = yf.shape[2]
    xg_blk = (pconv.shape[2] - LORA_G) // LORA_G
    lat = lambda i, j: (i, j, 0)
    seq = lambda i, j: (i, j + off, 0)
    const = lambda i, j: (0, 0)
    return pl.pallas_call(
        _merge_kernel,
        out_shape=jax.ShapeDtypeStruct((b, t, d), F32),
        grid=(b, t // tm),
        in_specs=[pl.BlockSpec((1, tm, d), lat),
                  pl.BlockSpec((1, tm, pool_y.shape[2]), lat),
                  pl.BlockSpec((1, tm, dr), seq),
                  pl.BlockSpec((1, tm, dr), seq),
                  pl.BlockSpec((1, tm, LORA_G), lambda i, j: (i, j + off, xg_blk)),
                  pl.BlockSpec((1, tm, gates.shape[2]), lat),
                  pl.BlockSpec((1, 8, d), lambda i, j: (i, 0, 0)),
                  pl.BlockSpec(g2.shape, const),
                  pl.BlockSpec(pool_proj.shape, const),
                  pl.BlockSpec(rwkv_proj.shape, const),
                  pl.BlockSpec(w_out.shape, const)],
        out_specs=pl.BlockSpec((1, tm, d), lat),
        compiler_params=pltpu.CompilerParams(dimension_semantics=("parallel", "parallel"),
                                             vmem_limit_bytes=VMEM_LIMIT),
        name="merge_out",
    )(x, pool_y, yf, yb, pconv, gates, mod, g2, pool_proj, rwkv_proj, w_out)


def _moe_kernel(h_ref, mod_ref, wr_ref, br_ref, wg_ref, wu_ref, wd_ref, o_ref, u_ref, gate_ref, acc_ref,
                *, n_exp, exp_blk):
    j = pl.program_id(1)
    lanes = gate_ref.shape[1]
    n_grp = n_exp // EXPERTS_PER_GROUP

    @pl.when(j == 0)
    def _():
        h = h_ref[...]
        u = h * _rms_scale(h) * mod_ref[0, 2:3, :] * (1.0 + mod_ref[0, 1:2, :]) + mod_ref[0, 0:1, :]
        u_ref[...] = u.astype(BF16)
        logits = _dot_f32(u, wr_ref[...]) + br_ref[...]
        lane_i = lax.broadcasted_iota(jnp.int32, logits.shape, 1)
        lane = lane_i.astype(F32)
        lane_grp = (lane_i // EXPERTS_PER_GROUP).astype(F32)
        neg = -jnp.inf
        big = float(lanes)
        is_grp = (lane_i >= n_exp) & (lane_i < n_exp + n_grp)
        gl = jnp.where(is_grp, logits, neg)
        gmax = jnp.max(gl, axis=-1, keepdims=True)
        p_group = 1.0 / jnp.sum(jnp.where(is_grp, jnp.exp(gl - gmax), 0.0), axis=-1, keepdims=True)
        g_idx = jnp.min(jnp.where(gl == gmax, lane - n_exp, big), axis=-1, keepdims=True)
        in_grp = (lane_i < n_exp) & (lane_grp == g_idx)
        el = jnp.where(in_grp, logits, neg)
        m1 = jnp.max(el, axis=-1, keepdims=True)
        i1 = jnp.min(jnp.where(el == m1, lane, big), axis=-1, keepdims=True)
        el2 = jnp.where(lane == i1, neg, el)
        m2 = jnp.max(el2, axis=-1, keepdims=True)
        i2 = jnp.min(jnp.where(el2 == m2, lane, big), axis=-1, keepdims=True)
        e21 = jnp.exp(m2 - m1)
        p1 = 1.0 / (1.0 + e21)
        p2 = e21 * p1
        gate_ref[...] = p_group * (jnp.where(lane == i1, p1, 0.0) + jnp.where(lane == i2, p2, 0.0))
        acc_ref[...] = jnp.zeros_like(acc_ref)

    u = u_ref[...]
    gates = gate_ref[...]
    lane = lax.broadcasted_iota(jnp.int32, gates.shape, 1)
    hs = []
    for e in range(exp_blk):
        ge = jnp.sum(jnp.where(lane == j * exp_blk + e, gates, 0.0), axis=-1, keepdims=True)
        hg = jnp.dot(u, wg_ref[e], preferred_element_type=F32)
        hu = jnp.dot(u, wu_ref[e], preferred_element_type=F32)
        hs.append((hg * _sigmoid(hg) * hu * ge).astype(BF16))
    hcat = jnp.concatenate(hs, axis=1)
    wd = wd_ref[...]
    acc_ref[...] += jnp.dot(hcat, wd.reshape(wd.shape[0] * wd.shape[1], wd.shape[2]), preferred_element_type=F32)

    @pl.when(j == pl.num_programs(1) - 1)
    def _():
        f = acc_ref[...]
        o_ref[...] = h_ref[...] + mod_ref[0, 3:4, :] * (f * _rms_scale(f) * mod_ref[0, 4:5, :])


def _moe(h1, mod, w_router, b_router, w_gate, w_up, w_down, tokens_per_batch):
    n, d = h1.shape
    n_exp, _, d_exp = w_gate.shape
    tm = min(1024, tokens_per_batch)
    exp_blk = 4
    assert n % tm == 0 and tokens_per_batch % tm == 0 and n_exp % exp_blk == 0
    per_b = tokens_per_batch // tm
    lanes = w_router.shape[1]
    return pl.pallas_call(
        functools.partial(_moe_kernel, n_exp=n_exp, exp_blk=exp_blk),
        out_shape=jax.ShapeDtypeStruct((n, d), F32),
        grid=(n // tm, n_exp // exp_blk),
        in_specs=[pl.BlockSpec((tm, d), lambda i, j: (i, 0)),
                  pl.BlockSpec((1, 8, d), lambda i, j: (i // per_b, 0, 0)),
                  pl.BlockSpec((d, lanes), lambda i, j: (0, 0)),
                  pl.BlockSpec((1, lanes), lambda i, j: (0, 0)),
                  pl.BlockSpec((exp_blk, d, d_exp), lambda i, j: (j, 0, 0)),
                  pl.BlockSpec((exp_blk, d, d_exp), lambda i, j: (j, 0, 0)),
                  pl.BlockSpec((exp_blk, d_exp, d), lambda i, j: (j, 0, 0))],
        out_specs=pl.BlockSpec((tm, d), lambda i, j: (i, 0)),
        scratch_shapes=[pltpu.VMEM((tm, d), BF16), pltpu.VMEM((tm, lanes), F32), pltpu.VMEM((tm, d), F32)],
        compiler_params=pltpu.CompilerParams(dimension_semantics=("parallel", "arbitrary"),
                                             vmem_limit_bytes=VMEM_LIMIT),
        name="moe",
    )(h1, mod, w_router, b_router, w_gate, w_up, w_down)


def _pad_rows(a, rows):
    return jnp.concatenate([a, jnp.zeros((rows - a.shape[0],) + a.shape[1:], a.dtype)], axis=0)


def _layer(l, h_lat, h_ctx, c, c_ctx, ada_w, ada_b, norm_gains, w_in, conv_w, pool_w, pool_scale, pool_proj,
           w0, w2, a0, a2, g2, k_k, k_a, r_k, lnx_w, lnx_b, rwkv_proj, w_out, router_group_w, router_group_b,
           router_expert_w, router_expert_b, expert_w_gate, expert_w_up, expert_w_down):
    b, t, d = h_lat.shape
    n_ctx = h_ctx.shape[1]
    d_pool = pool_scale.shape[1]
    d_rwkv = k_k.shape[1]
    d_conv = conv_w.shape[2]
    n_exp = expert_w_gate.shape[1]

    cc = _pad_rows(jnp.concatenate([c, c_ctx[None, :]], axis=0), -(-(b + 1) // 8) * 8)
    ada = _ada(cc, ada_w[l], ada_b[l][None, :])
    sh1, sc1, gt1, sh2, sc2, gt2 = [ada[:b, i * d:(i + 1) * d] for i in range(6)]
    csh1 = jnp.broadcast_to(ada[b, 0:d], (b, d))
    csc1 = jnp.broadcast_to(ada[b, d:2 * d], (b, d))
    gains = [jnp.broadcast_to(norm_gains[l, i], (b, d)) for i in range(4)]
    zero = jnp.zeros((b, d), F32)
    mod1 = jnp.stack([sh1, sc1, csh1, csc1, gains[0], gt1, gains[1], zero], axis=1)
    mod2 = jnp.stack([sh2, sc2, gains[2], gt2, gains[3], zero, zero, zero], axis=1)

    w_in_b = w_in[l].astype(BF16)
    w_pool = w_in_b[:, :d_pool]
    w_conv = w_in_b[:, d_pool:d_pool + d_conv]
    w_gate = w_in_b[:, d_pool + d_conv:]

    xcat = jnp.concatenate([h_ctx, h_lat], axis=1)
    pconv = _inproj_conv(xcat, mod1, w_conv, _pad_rows(conv_w[l], 8), n_ctx)
    pool_in, gates = _inproj_pg(h_lat, mod1, w_pool, w_gate)

    zw = jnp.zeros_like(w2[l, 0])
    za = jnp.zeros_like(a2[l, 0])
    w2p = jnp.stack([jnp.concatenate([w2[l, 0], zw], axis=0), jnp.concatenate([zw, w2[l, 1]], axis=0)]).astype(BF16)
    a2p = jnp.stack([jnp.concatenate([a2[l, 0], za], axis=0), jnp.concatenate([za, a2[l, 1]], axis=0)]).astype(BF16)
    vecs = _pad_rows(jnp.stack([w0[l, 0], w0[l, 1], a0[l, 0], a0[l, 1], k_k[l], k_a[l], r_k[l].reshape(-1),
                                lnx_w[l], lnx_b[l]]), 16)
    yf, yb = _rwkv(pconv, w2p, a2p, vecs, n_ctx, d_rwkv)

    pool_y = _pool(pool_in, pool_w[l], pool_scale[l])
    h1 = _merge(h_lat, pool_y, yf, yb, pconv, gates, mod1, g2[l].astype(BF16), pool_proj[l].astype(BF16),
                rwkv_proj[l].astype(BF16), w_out[l].astype(BF16), n_ctx)

    lanes = 128
    w_router = jnp.concatenate([router_expert_w[l], router_group_w[l],
                                jnp.zeros((d, lanes - n_exp - N_GROUPS), F32)], axis=1)
    b_router = jnp.concatenate([router_expert_b[l], router_group_b[l],
                                jnp.zeros((lanes - n_exp - N_GROUPS,), F32)])[None, :]
    h2 = _moe(h1.reshape(b * t, d), mod2, w_router, b_router, expert_w_gate[l].astype(BF16),
              expert_w_up[l].astype(BF16), expert_w_down[l].astype(BF16), t)
    return h2.reshape(b, t, d)


def kernel(x, c, ctx, c_ctx, ada_w, ada_b, norm_gains, w_in, conv_w, pool_w, pool_scale, pool_proj, w0, w2, a0, a2, g2, k_k, k_a, r_k, lnx_w, lnx_b, rwkv_proj, w_out, router_group_w, router_group_b, router_expert_w, router_expert_b, expert_w_gate, expert_w_up, expert_w_down):
    depth = ada_w.shape[0]
    assert depth == 1, "the context stream update between layers is not implemented"
    h_lat = x.astype(F32)
    h_ctx = ctx.astype(F32)
    h_lat = _layer(0, h_lat, h_ctx, c, c_ctx, ada_w, ada_b, norm_gains, w_in, conv_w, pool_w, pool_scale,
                   pool_proj, w0, w2, a0, a2, g2, k_k, k_a, r_k, lnx_w, lnx_b, rwkv_proj, w_out,
                   router_group_w, router_group_b, router_expert_w, router_expert_b,
                   expert_w_gate, expert_w_up, expert_w_down)
    return h_lat.astype(x.dtype)
```

```python
import functools
import math

import jax
import jax.numpy as jnp
from jax import lax
from jax.experimental import pallas as pl
from jax.experimental.pallas import tpu as pltpu

F32 = jnp.float32
BF16 = jnp.bfloat16

GRID_W = 64
POOL_WINDOWS = (2, 4, 8, 16)
POOL_GROUP_DIM = 128
HEAD_DIM = 64
LORA_W = 64
LORA_A = 64
LORA_G = 128
N_GROUPS = 4
EXPERTS_PER_GROUP = 8
GN_EPS = 64e-5
RMS_EPS = 1e-6
DECAY_OFFSET = 0.5

CHUNK = 64
HEADS_PER_LANE_GROUP = 4
LANE_GROUP = HEADS_PER_LANE_GROUP * HEAD_DIM
VMEM_LIMIT = 56 * 1024 * 1024


def _bdot(a, b):
    return jnp.dot(a.astype(BF16), b.astype(BF16), preferred_element_type=F32)


def _bdot_t(a, b):
    return lax.dot_general(a.astype(BF16), b.astype(BF16), (((1,), (1,)), ((), ())),
                           preferred_element_type=F32)


def _split3(a):
    hi = a.astype(BF16)
    r1 = a - hi.astype(F32)
    mid = r1.astype(BF16)
    lo = (r1 - mid.astype(F32)).astype(BF16)
    return hi, mid, lo


def _dot_f32(a, b):
    a0, a1, a2 = _split3(a)
    b0, b1, b2 = _split3(b)
    d = lambda p, q: jnp.dot(p, q, preferred_element_type=F32)
    return (d(a0, b0) + (d(a0, b1) + d(a1, b0))
            + (d(a0, b2) + d(a1, b1) + d(a2, b0)))


def _sigmoid(x):
    return 1.0 / (1.0 + jnp.exp(-x))


def _softplus(x):
    return jnp.maximum(x, 0.0) + jnp.log(1.0 + jnp.exp(-jnp.abs(x)))


def _rms_scale(x):
    return lax.rsqrt(jnp.mean(x * x, axis=-1, keepdims=True) + RMS_EPS)


def _ada_kernel(c_ref, w_ref, b_ref, o_ref):
    c = c_ref[...]
    o_ref[...] = _dot_f32(c * _sigmoid(c), w_ref[...]) + b_ref[...]


def _ada(cc, ada_w, ada_b):
    rows, d = cc.shape
    n = ada_w.shape[1]
    tn = 1024
    return pl.pallas_call(
        _ada_kernel,
        out_shape=jax.ShapeDtypeStruct((rows, n), F32),
        grid=(n // tn,),
        in_specs=[pl.BlockSpec((rows, d), lambda j: (0, 0)),
                  pl.BlockSpec((d, tn), lambda j: (0, j)),
                  pl.BlockSpec((1, tn), lambda j: (0, j))],
        out_specs=pl.BlockSpec((rows, tn), lambda j: (0, j)),
        compiler_params=pltpu.CompilerParams(dimension_semantics=("arbitrary",),
                                             vmem_limit_bytes=VMEM_LIMIT),
        name="ada",
    )(cc, ada_w, ada_b)


def _inproj_conv_kernel(x_ref, mod_ref, w_ref, cw_ref, o_ref, hn_ref, *, n_ctx, row_blk):
    seq = x_ref.shape[1]

    @pl.when(pl.program_id(1) == 0)
    def _():
        gain = mod_ref[0, 4:5, :]
        for r0 in range(0, seq, row_blk):
            is_ctx = r0 < n_ctx
            sh = mod_ref[0, 2:3, :] if is_ctx else mod_ref[0, 0:1, :]
            sc = mod_ref[0, 3:4, :] if is_ctx else mod_ref[0, 1:2, :]
            x = x_ref[0, r0:r0 + row_blk, :]
            hn = x * _rms_scale(x) * gain * (1.0 + sc) + sh
            hn_ref[r0:r0 + row_blk, :] = hn.astype(BF16)

    p = jnp.dot(hn_ref[...], w_ref[...], preferred_element_type=F32)
    row = lax.broadcasted_iota(jnp.int32, p.shape, 0)
    prev = pltpu.roll(p, 1, 0)
    prev = jnp.where((row == 0) | (row == n_ctx), 0.0, prev)
    nxt = pltpu.roll(p, seq - 1, 0)
    nxt = jnp.where((row == n_ctx - 1) | (row == seq - 1), 0.0, nxt)
    out = cw_ref[0:1, :] * prev + cw_ref[1:2, :] * p + cw_ref[2:3, :] * nxt
    o_ref[0] = out.astype(BF16)


def _inproj_conv(xcat, mod, w_conv, conv_w, n_ctx):
    b, seq, d = xcat.shape
    n = w_conv.shape[1]
    tn = 384
    row_blk = math.gcd(math.gcd(seq, n_ctx), 256)
    assert n % tn == 0 and seq % row_blk == 0 and n_ctx % row_blk == 0
    return pl.pallas_call(
        functools.partial(_inproj_conv_kernel, n_ctx=n_ctx, row_blk=row_blk),
        out_shape=jax.ShapeDtypeStruct((b, seq, n), BF16),
        grid=(b, n // tn),
        in_specs=[pl.BlockSpec((1, seq, d), lambda i, j: (i, 0, 0)),
                  pl.BlockSpec((1, 8, d), lambda i, j: (i, 0, 0)),
                  pl.BlockSpec((d, tn), lambda i, j: (0, j)),
                  pl.BlockSpec((8, tn), lambda i, j: (0, j))],
        out_specs=pl.BlockSpec((1, seq, tn), lambda i, j: (i, 0, j)),
        scratch_shapes=[pltpu.VMEM((seq, d), BF16)],
        compiler_params=pltpu.CompilerParams(dimension_semantics=("parallel", "arbitrary"),
                                             vmem_limit_bytes=VMEM_LIMIT),
        name="inproj_conv",
    )(xcat, mod, w_conv, conv_w)


def _inproj_pg_kernel(x_ref, mod_ref, wp_ref, wg_ref, pool_ref, gate_ref):
    x = x_ref[0]
    hn = (x * _rms_scale(x) * mod_ref[0, 4:5, :] * (1.0 + mod_ref[0, 1:2, :]) + mod_ref[0, 0:1, :]).astype(BF16)
    pool_ref[0] = jnp.dot(hn, wp_ref[...], preferred_element_type=F32).astype(BF16)
    gate_ref[0] = _sigmoid(jnp.dot(hn, wg_ref[...], preferred_element_type=F32)).astype(BF16)


def _inproj_pg(x, mod, w_pool, w_gate):
    b, t, d = x.shape
    tm = min(512, t)
    assert t % tm == 0
    npool, ngate = w_pool.shape[1], w_gate.shape[1]
    return pl.pallas_call(
        _inproj_pg_kernel,
        out_shape=(jax.ShapeDtypeStruct((b, t, npool), BF16), jax.ShapeDtypeStruct((b, t, ngate), BF16)),
        grid=(b, t // tm),
        in_specs=[pl.BlockSpec((1, tm, d), lambda i, j: (i, j, 0)),
                  pl.BlockSpec((1, 8, d), lambda i, j: (i, 0, 0)),
                  pl.BlockSpec((d, npool), lambda i, j: (0, 0)),
                  pl.BlockSpec((d, ngate), lambda i, j: (0, 0))],
        out_specs=(pl.BlockSpec((1, tm, npool), lambda i, j: (i, j, 0)),
                   pl.BlockSpec((1, tm, ngate), lambda i, j: (i, j, 0))),
        compiler_params=pltpu.CompilerParams(dimension_semantics=("parallel", "parallel"),
                                             vmem_limit_bytes=VMEM_LIMIT),
        name="inproj_pool_gate",
    )(x, mod, w_pool, w_gate)


def _stack(x, bd):
    xb = x.astype(BF16)
    return jnp.concatenate([xb] * HEADS_PER_LANE_GROUP, axis=0) * bd


def _rwkv_kernel(xf_ref, xb_ref, w2_ref, a2_ref, vec_ref, bd_ref, icat_ref, mask_ref, tri_ref,
                 of_ref, ob_ref, s_ref, *, d_rwkv):
    step = pl.program_id(1)

    @pl.when(step == 0)
    def _():
        s_ref[...] = jnp.zeros_like(s_ref)

    bd = bd_ref[...]
    icat = icat_ref[...]
    n_lane_groups = d_rwkv // LANE_GROUP
    o_xw = 3 * d_rwkv
    o_xa = o_xw + 2 * LORA_W

    def bmm(a, b):
        return jnp.dot(a.astype(BF16), _stack(b, bd), preferred_element_type=F32)

    probs = [(d, g) for d in range(2) for g in range(n_lane_groups)]
    x_refs = (xf_ref, xb_ref)
    o_refs = (of_ref, ob_ref)
    C = CHUNK
    icat_b = icat.astype(BF16)

    def lanes(g):
        return slice(g * LANE_GROUP, (g + 1) * LANE_GROUP)

    def each(fn, *lists):
        return [fn(*args) for args in zip(*lists)]

    def split_rows(x, n):
        return [x[i * C:(i + 1) * C] for i in range(n)]

    def shared_bd_dot(blocks):
        return split_rows(_bdot(jnp.concatenate(blocks, axis=0), bd), len(blocks))

    tanh_xw = [jnp.tanh(x_refs[d][0, :, o_xw:o_xw + 2 * LORA_W].astype(F32)).astype(BF16) for d in range(2)]
    xa = [x_refs[d][0, :, o_xa:o_xa + 2 * LORA_A] for d in range(2)]
    r = [x_refs[d][0, :, lanes(g)].astype(F32) for d, g in probs]
    k = [x_refs[d][0, :, d_rwkv + g * LANE_GROUP:d_rwkv + (g + 1) * LANE_GROUP].astype(F32) for d, g in probs]
    v = [x_refs[d][0, :, 2 * d_rwkv + g * LANE_GROUP:2 * d_rwkv + (g + 1) * LANE_GROUP].astype(F32)
         for d, g in probs]

    z = [vec_ref[d:d + 1, lanes(g)] + jnp.dot(tanh_xw[d], w2_ref[d, :, lanes(g)], preferred_element_type=F32)
         for d, g in probs]
    lw = [-jnp.exp(-_softplus(-zi) - DECAY_OFFSET) for zi in z]
    a = [_sigmoid(vec_ref[2 + d:3 + d, lanes(g)]
                  + jnp.dot(xa[d], a2_ref[d, :, lanes(g)], preferred_element_type=F32)) for d, g in probs]
    kkr = [ki * vec_ref[4:5, lanes(g)] for ki, (d, g) in zip(k, probs)]
    kd = [ki * (1.0 + (ai - 1.0) * vec_ref[5:6, lanes(g)]) for ki, ai, (d, g) in zip(k, a, probs)]
    rkr = [ri * kdi * vec_ref[6:7, lanes(g)] for ri, kdi, (d, g) in zip(r, kd, probs)]
    sums = shared_bd_dot([x * x for x in kkr] + rkr)
    kk = [x / jnp.maximum(jnp.sqrt(ss), 1e-12) for x, ss in zip(kkr, sums[:len(probs)])]
    bonus = [bs * vi for bs, vi in zip(sums[len(probs):], v)]

    def cumdecay(lwi, d):
        l0, l1, l2 = _split3(lwi)
        cs = jnp.dot(tri_ref[d], jnp.concatenate([l0, l1, l2], axis=1), preferred_element_type=F32)
        return cs[:, :LANE_GROUP] + cs[:, LANE_GROUP:2 * LANE_GROUP] + cs[:, 2 * LANE_GROUP:]

    cl = [cumdecay(lwi, d) for lwi, (d, g) in zip(lw, probs)]
    e_pos = [jnp.exp(x) for x in cl]
    e_neg = [jnp.exp(-x) for x in cl]
    at = [-kki * jnp.exp(cli - lwi) for kki, cli, lwi in zip(kk, cl, lw)]
    rt = each(lambda ri, e: ri * e, r, e_pos)
    bt = each(lambda kki, ai, e: kki * ai * e, kk, a, e_neg)
    kt = each(lambda kdi, e: kdi * e, kd, e_neg)

    lhs = each(lambda ai, ri: jnp.concatenate([ai, ri, icat], axis=0).astype(BF16), at, rt)
    gb = each(lambda l, x: _bdot_t(l, _stack(x, bd)), lhs, bt)
    gk = each(lambda l, x: _bdot_t(l, _stack(x, bd)), lhs, kt)
    strict = [mask_ref[d, 0] for d, g in probs]
    incl = [mask_ref[d, 1] for d, g in probs]
    a_ab = each(lambda x, m: x[0:C] * m, gb, strict)
    a_rb = each(lambda x, m: x[C:2 * C] * m, gb, incl)
    bt_t = [x[2 * C:3 * C] for x in gb]
    a_ak = each(lambda x, m: x[0:C] * m, gk, strict)
    a_rk = each(lambda x, m: x[C:2 * C] * m, gk, incl)
    kt_t = [x[2 * C:3 * C] for x in gk]

    tinv = [icat + x for x in a_ab]
    lp = a_ab
    for _ in range(5):
        lp = each(bmm, lp, lp)
        tinv = each(lambda t, l: t + bmm(t, l), tinv, lp)

    hv = each(lambda p, q_, t_, vi: jnp.dot(jnp.concatenate([p, q_, t_], axis=0).astype(BF16), _stack(vi, bd),
                                            preferred_element_type=F32), a_ak, a_rk, kt_t, v)
    wt = each(bmm, tinv, at)
    ut = each(lambda t, h: bmm(t, h[0:C]), tinv, hv)
    lhs2 = each(lambda p, q_: jnp.concatenate([p, q_], axis=0).astype(BF16), a_rb, bt_t)
    hw = each(lambda l, x: jnp.dot(l, _stack(x, bd), preferred_element_type=F32), lhs2, wt)
    hu = each(lambda l, x: jnp.dot(l, _stack(x, bd), preferred_element_type=F32), lhs2, ut)

    def gam_rows(e, d):
        last = C - 1 if d == 0 else 0
        gam = e[last:last + 1, :]
        gam_hi = gam.astype(BF16)
        gam_lo = (gam - gam_hi.astype(F32)).astype(BF16)
        return [icat_b * gam_hi, icat_b * gam_lo]

    gparts = split_rows(jnp.dot(jnp.concatenate([blk for e, (d, g) in zip(e_pos, probs) for blk in gam_rows(e, d)],
                                                axis=0), bd, preferred_element_type=F32), 2 * len(probs))
    gcat = [gparts[2 * i] + gparts[2 * i + 1] for i in range(len(probs))]

    s0 = [s_ref[d, g] for d, g in probs]
    hs = each(lambda ri, hwi, s: jnp.dot(jnp.concatenate([ri + hwi[0:C], hwi[C:2 * C]], axis=0).astype(BF16),
                                         _stack(s, bd), preferred_element_type=F32), rt, hw, s0)
    y = each(lambda hsi, hui, hvi: hsi[0:C] + hui[0:C] + hvi[C:2 * C], hs, hu, hv)
    for (d, g), gc, s, hsi, hui, hvi in zip(probs, gcat, s0, hs, hu, hv):
        s_ref[d, g] = gc * (s + hsi[C:2 * C] + hui[C:2 * C] + hvi[2 * C:3 * C])

    inv_n = 1.0 / HEAD_DIM
    mu = [m * inv_n for m in shared_bd_dot(y)]
    yc = each(lambda yi, m: yi - m, y, mu)
    var = [m * inv_n for m in shared_bd_dot([x * x for x in yc])]
    for (d, g), yci, vari, bi in zip(probs, yc, var, bonus):
        out = yci * lax.rsqrt(vari + GN_EPS) * vec_ref[7:8, lanes(g)] + vec_ref[8:9, lanes(g)] + bi
        o_refs[d][0, :, lanes(g)] = out.astype(BF16)


def _rwkv(pconv, w2p, a2p, vecs, n_ctx, d_rwkv):
    b, seq, n = pconv.shape
    nc = seq // CHUNK
    nc_ctx = n_ctx // CHUNK
    assert seq % CHUNK == 0 and n_ctx % CHUNK == 0 and d_rwkv % LANE_GROUP == 0
    n_lane_groups = d_rwkv // LANE_GROUP

    lane = jnp.arange(LANE_GROUP)
    rowi = jnp.arange(CHUNK)
    j = (lane % HEAD_DIM)[None, :]
    t = rowi[:, None]
    bd = (lane[:, None] // HEAD_DIM == lane[None, :] // HEAD_DIM).astype(BF16)
    icat = (j == t).astype(F32)
    masks = jnp.stack([jnp.stack([(j < t), (j <= t)]), jnp.stack([(j > t), (j >= t)])]).astype(F32)
    tri = jnp.stack([rowi[None, :] <= rowi[:, None], rowi[None, :] >= rowi[:, None]]).astype(BF16)

    def bwd_chunk(s):
        return jnp.where(s < nc_ctx, nc_ctx - 1 - s, nc + nc_ctx - 1 - s)

    const2 = lambda i, s: (0, 0)
    const3 = lambda i, s: (0, 0, 0)
    const4 = lambda i, s: (0, 0, 0, 0)
    return pl.pallas_call(
        functools.partial(_rwkv_kernel, d_rwkv=d_rwkv),
        out_shape=(jax.ShapeDtypeStruct((b, seq, d_rwkv), BF16), jax.ShapeDtypeStruct((b, seq, d_rwkv), BF16)),
        grid=(b, nc),
        in_specs=[pl.BlockSpec((1, CHUNK, n), lambda i, s: (i, s, 0)),
                  pl.BlockSpec((1, CHUNK, n), lambda i, s: (i, bwd_chunk(s), 0)),
                  pl.BlockSpec(w2p.shape, const3),
                  pl.BlockSpec(a2p.shape, const3),
                  pl.BlockSpec(vecs.shape, const2),
                  pl.BlockSpec(bd.shape, const2),
                  pl.BlockSpec(icat.shape, const2),
                  pl.BlockSpec(masks.shape, const4),
                  pl.BlockSpec(tri.shape, const3)],
        out_specs=(pl.BlockSpec((1, CHUNK, d_rwkv), lambda i, s: (i, s, 0)),
                   pl.BlockSpec((1, CHUNK, d_rwkv), lambda i, s: (i, bwd_chunk(s), 0))),
        scratch_shapes=[pltpu.VMEM((2, n_lane_groups, CHUNK, LANE_GROUP), F32)],
        compiler_params=pltpu.CompilerParams(dimension_semantics=("parallel", "arbitrary"),
                                             vmem_limit_bytes=VMEM_LIMIT),
        name="rwkv_scan",
    )(pconv, pconv, w2p, a2p, vecs, bd, icat, masks, tri)


def _pool_kernel(u_ref, win_ref, inv_ref, w_ref, sc_ref, o_ref):
    u = u_ref[0]
    mean = jnp.dot(win_ref[0], u, preferred_element_type=F32) * inv_ref[0]
    diff = mean - u.astype(F32)
    o_ref[0] = (_bdot(diff, w_ref[0]) * sc_ref[0]).astype(BF16)


def _pool(pool_in, pool_w, pool_scale):
    b, t, dp = pool_in.shape
    ng = len(POOL_WINDOWS)
    cg = POOL_GROUP_DIM
    assert dp == ng * cg and t % GRID_W == 0
    tok = jnp.arange(t)
    row, col = tok // GRID_W, tok % GRID_W
    wins, invs = [], []
    for win in POOL_WINDOWS:
        lo = win // 2
        hi = win - lo - 1
        dr = row[None, :] - row[:, None]
        dc = col[None, :] - col[:, None]
        m = (dr >= -lo) & (dr <= hi) & (dc >= -lo) & (dc <= hi)
        wins.append(m.astype(BF16))
        invs.append(1.0 / jnp.sum(m, axis=1, dtype=F32))
    wins = jnp.stack(wins)
    invs = jnp.broadcast_to(jnp.stack(invs)[:, :, None], (ng, t, cg))
    return pl.pallas_call(
        _pool_kernel,
        out_shape=jax.ShapeDtypeStruct((b, t, dp), BF16),
        grid=(ng, b),
        in_specs=[pl.BlockSpec((1, t, cg), lambda g, i: (i, 0, g)),
                  pl.BlockSpec((1, t, t), lambda g, i: (g, 0, 0)),
                  pl.BlockSpec((1, t, cg), lambda g, i: (g, 0, 0)),
                  pl.BlockSpec((1, cg, cg), lambda g, i: (g, 0, 0)),
                  pl.BlockSpec((1, 1, cg), lambda g, i: (g, 0, 0))],
        out_specs=pl.BlockSpec((1, t, cg), lambda g, i: (i, 0, g)),
        compiler_params=pltpu.CompilerParams(dimension_semantics=("arbitrary", "arbitrary"),
                                             vmem_limit_bytes=VMEM_LIMIT),
        name="pool_mix",
    )(pool_in, wins, invs, pool_w.astype(BF16), pool_scale.reshape(ng, 1, cg))


def _merge_kernel(x_ref, py_ref, yf_ref, yb_ref, xg_ref, gate_ref, mod_ref, g2_ref, pp_ref, rp_ref, wo_ref, o_ref):
    d = x_ref.shape[2]
    g = jnp.dot(_sigmoid(xg_ref[0].astype(F32)).astype(BF16), g2_ref[...], preferred_element_type=F32)
    ry = (yf_ref[0].astype(F32) + yb_ref[0].astype(F32)) * g
    gates = gate_ref[0]
    m = (gates[:, :d].astype(F32) * jnp.dot(py_ref[0], pp_ref[...], preferred_element_type=F32)
         + gates[:, d:].astype(F32) * _bdot(ry, rp_ref[...]))
    mo = _bdot(m, wo_ref[...])
    o_ref[0] = x_ref[0] + mod_ref[0, 5:6, :] * (mo * _rms_scale(mo) * mod_ref[0, 6:7, :])


def _merge(x, pool_y, yf, yb, pconv, gates, mod, g2, pool_proj, rwkv_proj, w_out, n_ctx):
    b, t, d = x.shape
    tm = math.gcd(math.gcd(t, n_ctx), 256)
    off = n_ctx // tm
    dr = yf.shape[2]
    xg_blk = (pconv.shape[2] - LORA_G) // LORA_G
    lat = lambda i, j: (i, j, 0)
    seq = lambda i, j: (i, j + off, 0)
    const = lambda i, j: (0, 0)
    return pl.pallas_call(
        _merge_kernel,
        out_shape=jax.ShapeDtypeStruct((b, t, d), F32),
        grid=(b, t // tm),
        in_specs=[pl.BlockSpec((1, tm, d), lat),
                  pl.BlockSpec((1, tm, pool_y.shape[2]), lat),
                  pl.BlockSpec((1, tm, dr), seq),
                  pl.BlockSpec((1, tm, dr), seq),
                  pl.BlockSpec((1, tm, LORA_G), lambda i, j: (i, j + off, xg_blk)),
                  pl.BlockSpec((1, tm, gates.shape[2]), lat),
                  pl.BlockSpec((1, 8, d), lambda i, j: (i, 0, 0)),
                  pl.BlockSpec(g2.shape, const),
                  pl.BlockSpec(pool_proj.shape, const),
                  pl.BlockSpec(rwkv_proj.shape, const),
                  pl.BlockSpec(w_out.shape, const)],
        out_specs=pl.BlockSpec((1, tm, d), lat),
        compiler_params=pltpu.CompilerParams(dimension_semantics=("parallel", "parallel"),
                                             vmem_limit_bytes=VMEM_LIMIT),
        name="merge_out",
    )(x, pool_y, yf, yb, pconv, gates, mod, g2, pool_proj, rwkv_proj, w_out)


def _moe_kernel(h_ref, mod_ref, wr_ref, br_ref, wg_ref, wu_ref, wd_ref, o_ref, u_ref, gate_ref, acc_ref,
                *, n_exp, exp_blk):
    j = pl.program_id(1)
    lanes = gate_ref.shape[1]
    n_grp = n_exp // EXPERTS_PER_GROUP

    @pl.when(j == 0)
    def _():
        h = h_ref[...]
        u = h * _rms_scale(h) * mod_ref[0, 2:3, :] * (1.0 + mod_ref[0, 1:2, :]) + mod_ref[0, 0:1, :]
        u_ref[...] = u.astype(BF16)
        logits = _dot_f32(u, wr_ref[...]) + br_ref[...]
        lane_i = lax.broadcasted_iota(jnp.int32, logits.shape, 1)
        lane = lane_i.astype(F32)
        lane_grp = (lane_i // EXPERTS_PER_GROUP).astype(F32)
        neg = -jnp.inf
        big = float(lanes)
        is_grp = (lane_i >= n_exp) & (lane_i < n_exp + n_grp)
        gl = jnp.where(is_grp, logits, neg)
        gmax = jnp.max(gl, axis=-1, keepdims=True)
        p_group = 1.0 / jnp.sum(jnp.where(is_grp, jnp.exp(gl - gmax), 0.0), axis=-1, keepdims=True)
        g_idx = jnp.min(jnp.where(gl == gmax, lane - n_exp, big), axis=-1, keepdims=True)
        in_grp = (lane_i < n_exp) & (lane_grp == g_idx)
        el = jnp.where(in_grp, logits, neg)
        m1 = jnp.max(el, axis=-1, keepdims=True)
        i1 = jnp.min(jnp.where(el == m1, lane, big), axis=-1, keepdims=True)
        el2 = jnp.where(lane == i1, neg, el)
        m2 = jnp.max(el2, axis=-1, keepdims=True)
        i2 = jnp.min(jnp.where(el2 == m2, lane, big), axis=-1, keepdims=True)
        e21 = jnp.exp(m2 - m1)
        p1 = 1.0 / (1.0 + e21)
        p2 = e21 * p1
        gate_ref[...] = p_group * (jnp.where(lane == i1, p1, 0.0) + jnp.where(lane == i2, p2, 0.0))
        acc_ref[...] = jnp.zeros_like(acc_ref)

    u = u_ref[...]
    gates = gate_ref[...]
    lane = lax.broadcasted_iota(jnp.int32, gates.shape, 1)
    hs = []
    for e in range(exp_blk):
        ge = jnp.sum(jnp.where(lane == j * exp_blk + e, gates, 0.0), axis=-1, keepdims=True)
        hg = jnp.dot(u, wg_ref[e], preferred_element_type=F32)
        hu = jnp.dot(u, wu_ref[e], preferred_element_type=F32)
        hs.append((hg * _sigmoid(hg) * hu * ge).astype(BF16))
    hcat = jnp.concatenate(hs, axis=1)
    wd = wd_ref[...]
    acc_ref[...] += jnp.dot(hcat, wd.reshape(wd.shape[0] * wd.shape[1], wd.shape[2]), preferred_element_type=F32)

    @pl.when(j == pl.num_programs(1) - 1)
    def _():
        f = acc_ref[...]
        o_ref[...] = h_ref[...] + mod_ref[0, 3:4, :] * (f * _rms_scale(f) * mod_ref[0, 4:5, :])


def _moe(h1, mod, w_router, b_router, w_gate, w_up, w_down, tokens_per_batch):
    n, d = h1.shape
    n_exp, _, d_exp = w_gate.shape
    tm = min(1024, tokens_per_batch)
    exp_blk = 4
    assert n % tm == 0 and tokens_per_batch % tm == 0 and n_exp % exp_blk == 0
    per_b = tokens_per_batch // tm
    lanes = w_router.shape[1]
    return pl.pallas_call(
        functools.partial(_moe_kernel, n_exp=n_exp, exp_blk=exp_blk),
        out_shape=jax.ShapeDtypeStruct((n, d), F32),
        grid=(n // tm, n_exp // exp_blk),
        in_specs=[pl.BlockSpec((tm, d), lambda i, j: (i, 0)),
                  pl.BlockSpec((1, 8, d), lambda i, j: (i // per_b, 0, 0)),
                  pl.BlockSpec((d, lanes), lambda i, j: (0, 0)),
                  pl.BlockSpec((1, lanes), lambda i, j: (0, 0)),
                  pl.BlockSpec((exp_blk, d, d_exp), lambda i, j: (j, 0, 0)),
                  pl.BlockSpec((exp_blk, d, d_exp), lambda i, j: (j, 0, 0)),
                  pl.BlockSpec((exp_blk, d_exp, d), lambda i, j: (j, 0, 0))],
        out_specs=pl.BlockSpec((tm, d), lambda i, j: (i, 0)),
        scratch_shapes=[pltpu.VMEM((tm, d), BF16), pltpu.VMEM((tm, lanes), F32), pltpu.VMEM((tm, d), F32)],
        compiler_params=pltpu.CompilerParams(dimension_semantics=("parallel", "arbitrary"),
                                             vmem_limit_bytes=VMEM_LIMIT),
        name="moe",
    )(h1, mod, w_router, b_router, w_gate, w_up, w_down)


def _pad_rows(a, rows):
    return jnp.concatenate([a, jnp.zeros((rows - a.shape[0],) + a.shape[1:], a.dtype)], axis=0)


def _layer(l, h_lat, h_ctx, c, c_ctx, ada_w, ada_b, norm_gains, w_in, conv_w, pool_w, pool_scale, pool_proj,
           w0, w2, a0, a2, g2, k_k, k_a, r_k, lnx_w, lnx_b, rwkv_proj, w_out, router_group_w, router_group_b,
           router_expert_w, router_expert_b, expert_w_gate, expert_w_up, expert_w_down):
    b, t, d = h_lat.shape
    n_ctx = h_ctx.shape[1]
    d_pool = pool_scale.shape[1]
    d_rwkv = k_k.shape[1]
    d_conv = conv_w.shape[2]
    n_exp = expert_w_gate.shape[1]

    cc = _pad_rows(jnp.concatenate([c, c_ctx[None, :]], axis=0), -(-(b + 1) // 8) * 8)
    ada = _ada(cc, ada_w[l], ada_b[l][None, :])
    sh1, sc1, gt1, sh2, sc2, gt2 = [ada[:b, i * d:(i + 1) * d] for i in range(6)]
    csh1 = jnp.broadcast_to(ada[b, 0:d], (b, d))
    csc1 = jnp.broadcast_to(ada[b, d:2 * d], (b, d))
    gains = [jnp.broadcast_to(norm_gains[l, i], (b, d)) for i in range(4)]
    zero = jnp.zeros((b, d), F32)
    mod1 = jnp.stack([sh1, sc1, csh1, csc1, gains[0], gt1, gains[1], zero], axis=1)
    mod2 = jnp.stack([sh2, sc2, gains[2], gt2, gains[3], zero, zero, zero], axis=1)

    w_in_b = w_in[l].astype(BF16)
    w_pool = w_in_b[:, :d_pool]
    w_conv = w_in_b[:, d_pool:d_pool + d_conv]
    w_gate = w_in_b[:, d_pool + d_conv:]

    xcat = jnp.concatenate([h_ctx, h_lat], axis=1)
    pconv = _inproj_conv(xcat, mod1, w_conv, _pad_rows(conv_w[l], 8), n_ctx)
    pool_in, gates = _inproj_pg(h_lat, mod1, w_pool, w_gate)

    zw = jnp.zeros_like(w2[l, 0])
    za = jnp.zeros_like(a2[l, 0])
    w2p = jnp.stack([jnp.concatenate([w2[l, 0], zw], axis=0), jnp.concatenate([zw, w2[l, 1]], axis=0)]).astype(BF16)
    a2p = jnp.stack([jnp.concatenate([a2[l, 0], za], axis=0), jnp.concatenate([za, a2[l, 1]], axis=0)]).astype(BF16)
    vecs = _pad_rows(jnp.stack([w0[l, 0], w0[l, 1], a0[l, 0], a0[l, 1], k_k[l], k_a[l], r_k[l].reshape(-1),
                                lnx_w[l], lnx_b[l]]), 16)
    yf, yb = _rwkv(pconv, w2p, a2p, vecs, n_ctx, d_rwkv)

    pool_y = _pool(pool_in, pool_w[l], pool_scale[l])
    h1 = _merge(h_lat, pool_y, yf, yb, pconv, gates, mod1, g2[l].astype(BF16), pool_proj[l].astype(BF16),
                rwkv_proj[l].astype(BF16), w_out[l].astype(BF16), n_ctx)

    lanes = 128
    w_router = jnp.concatenate([router_expert_w[l], router_group_w[l],
                                jnp.zeros((d, lanes - n_exp - N_GROUPS), F32)], axis=1)
    b_router = jnp.concatenate([router_expert_b[l], router_group_b[l],
                                jnp.zeros((lanes - n_exp - N_GROUPS,), F32)])[None, :]
    h2 = _moe(h1.reshape(b * t, d), mod2, w_router, b_router, expert_w_gate[l].astype(BF16),
              expert_w_up[l].astype(BF16), expert_w_down[l].astype(BF16), t)
    return h2.reshape(b, t, d)


def kernel(x, c, ctx, c_ctx, ada_w, ada_b, norm_gains, w_in, conv_w, pool_w, pool_scale, pool_proj, w0, w2, a0, a2, g2, k_k, k_a, r_k, lnx_w, lnx_b, rwkv_proj, w_out, router_group_w, router_group_b, router_expert_w, router_expert_b, expert_w_gate, expert_w_up, expert_w_down):
    depth = ada_w.shape[0]
    assert depth == 1, "the context stream update between layers is not implemented"
    h_lat = x.astype(F32)
    h_ctx = ctx.astype(F32)
    h_lat = _layer(0, h_lat, h_ctx, c, c_ctx, ada_w, ada_b, norm_gains, w_in, conv_w, pool_w, pool_scale,
                   pool_proj, w0, w2, a0, a2, g2, k_k, k_a, r_k, lnx_w, lnx_b, rwkv_proj, w_out,
                   router_group_w, router_group_b, router_expert_w, router_expert_b,
                   expert_w_gate, expert_w_up, expert_w_down)
    return h_lat.astype(x.dtype)
```

```python
import functools
import math

import jax
import jax.numpy as jnp
from jax import lax
from jax.experimental import pallas as pl
from jax.experimental.pallas import tpu as pltpu

F32 = jnp.float32
BF16 = jnp.bfloat16

GRID_W = 64
POOL_WINDOWS = (2, 4, 8, 16)
POOL_GROUP_DIM = 128
HEAD_DIM = 64
LORA_W = 64
LORA_A = 64
LORA_G = 128
N_GROUPS = 4
EXPERTS_PER_GROUP = 8
GN_EPS = 64e-5
RMS_EPS = 1e-6
DECAY_OFFSET = 0.5

CHUNK = 64
HEADS_PER_LANE_GROUP = 4
LANE_GROUP = HEADS_PER_LANE_GROUP * HEAD_DIM
MOE_SLOT_BLOCK = 128
VMEM_LIMIT = 56 * 1024 * 1024


def _bdot(a, b):
    return jnp.dot(a.astype(BF16), b.astype(BF16), preferred_element_type=F32)


def _bdot_t(a, b):
    return lax.dot_general(a.astype(BF16), b.astype(BF16), (((1,), (1,)), ((), ())),
                           preferred_element_type=F32)


def _split3(a):
    hi = a.astype(BF16)
    r1 = a - hi.astype(F32)
    mid = r1.astype(BF16)
    lo = (r1 - mid.astype(F32)).astype(BF16)
    return hi, mid, lo


def _dot_f32(a, b):
    a0, a1, a2 = _split3(a)
    b0, b1, b2 = _split3(b)
    d = lambda p, q: jnp.dot(p, q, preferred_element_type=F32)
    return (d(a0, b0) + (d(a0, b1) + d(a1, b0))
            + (d(a0, b2) + d(a1, b1) + d(a2, b0)))


def _sigmoid(x):
    return 1.0 / (1.0 + jnp.exp(-x))


def _softplus(x):
    return jnp.maximum(x, 0.0) + jnp.log(1.0 + jnp.exp(-jnp.abs(x)))


def _rms_scale(x):
    return lax.rsqrt(jnp.mean(x * x, axis=-1, keepdims=True) + RMS_EPS)


def _ada_kernel(c_ref, w_ref, b_ref, o_ref):
    c = c_ref[...]
    o_ref[...] = _dot_f32(c * _sigmoid(c), w_ref[...]) + b_ref[...]


def _ada(cc, ada_w, ada_b):
    rows, d = cc.shape
    n = ada_w.shape[1]
    tn = 1024
    return pl.pallas_call(
        _ada_kernel,
        out_shape=jax.ShapeDtypeStruct((rows, n), F32),
        grid=(n // tn,),
        in_specs=[pl.BlockSpec((rows, d), lambda j: (0, 0)),
                  pl.BlockSpec((d, tn), lambda j: (0, j)),
                  pl.BlockSpec((1, tn), lambda j: (0, j))],
        out_specs=pl.BlockSpec((rows, tn), lambda j: (0, j)),
        compiler_params=pltpu.CompilerParams(dimension_semantics=("arbitrary",),
                                             vmem_limit_bytes=VMEM_LIMIT),
        name="ada",
    )(cc, ada_w, ada_b)


def _inproj_conv_kernel(c_ref, x_ref, mod_ref, w_ref, cw_ref, o_ref, xg_ref, hn_ref, *, row_blk, lat_blk, xg_col):
    n_ctx = c_ref.shape[1]
    seq = n_ctx + x_ref.shape[1]

    @pl.when(pl.program_id(1) == 0)
    def _():
        gain = mod_ref[0, 4:5, :]
        for r0 in range(0, seq, row_blk):
            is_ctx = r0 < n_ctx
            sh = mod_ref[0, 2:3, :] if is_ctx else mod_ref[0, 0:1, :]
            sc = mod_ref[0, 3:4, :] if is_ctx else mod_ref[0, 1:2, :]
            x = c_ref[0, r0:r0 + row_blk, :] if is_ctx else x_ref[0, r0 - n_ctx:r0 - n_ctx + row_blk, :]
            hn = x * _rms_scale(x) * gain * (1.0 + sc) + sh
            hn_ref[r0:r0 + row_blk, :] = hn.astype(BF16)

    def conv(q, first_row, masked):
        n = q.shape[0]
        prev = pltpu.roll(q, 1, 0)
        nxt = pltpu.roll(q, n - 1, 0)
        if masked:
            row = lax.broadcasted_iota(jnp.int32, q.shape, 0) + first_row
            prev = jnp.where((row == 0) | (row == n_ctx), 0.0, prev)
            nxt = jnp.where((row == n_ctx - 1) | (row == seq - 1), 0.0, nxt)
        return (cw_ref[0:1, :] * prev + cw_ref[1:2, :] * q + cw_ref[2:3, :] * nxt).astype(BF16)

    halo = 16
    blocks = [(0, n_ctx)] + [(r, r + lat_blk) for r in range(n_ctx, seq, lat_blk)]
    for r0, r1 in blocks:
        at_start = r0 in (0, n_ctx)
        at_end = r1 in (n_ctx, seq)
        w0 = r0 if at_start else r0 - halo
        w1 = r1 if at_end else r1 + halo
        q = jnp.dot(hn_ref[w0:w1, :], w_ref[...], preferred_element_type=F32)
        o_ref[0, r0:r1, :] = conv(q, w0, False)[r0 - w0:r1 - w0]
        if at_start:
            o_ref[0, r0:r0 + halo, :] = conv(q[0:2 * halo], w0, True)[0:halo]
        if at_end:
            o_ref[0, r1 - halo:r1, :] = conv(q[w1 - w0 - 2 * halo:], w1 - 2 * halo, True)[halo:]

    tn = o_ref.shape[2]

    @pl.when(pl.program_id(1) == xg_col // tn)
    def _():
        xg_ref[0] = o_ref[0, n_ctx:, xg_col % tn:xg_col % tn + LORA_G]


def _inproj_conv(h_ctx, h_lat, mod, w_conv, conv_w, xg_col):
    b, t, d = h_lat.shape
    n_ctx = h_ctx.shape[1]
    seq = n_ctx + t
    n = w_conv.shape[1]
    tn = 512
    row_blk = math.gcd(math.gcd(t, n_ctx), 256)
    lat_blk = math.gcd(t, 512)
    assert n % tn == 0 and xg_col % LORA_G == 0 and min(lat_blk, n_ctx) >= 32 and n_ctx % 16 == 0
    return pl.pallas_call(
        functools.partial(_inproj_conv_kernel, row_blk=row_blk, lat_blk=lat_blk, xg_col=xg_col),
        out_shape=(jax.ShapeDtypeStruct((b, seq, n), BF16), jax.ShapeDtypeStruct((b, t, LORA_G), BF16)),
        grid=(b, n // tn),
        in_specs=[pl.BlockSpec((1, n_ctx, d), lambda i, j: (i, 0, 0)),
                  pl.BlockSpec((1, t, d), lambda i, j: (i, 0, 0)),
                  pl.BlockSpec((1, 8, d), lambda i, j: (i, 0, 0)),
                  pl.BlockSpec((d, tn), lambda i, j: (0, j)),
                  pl.BlockSpec((8, tn), lambda i, j: (0, j))],
        out_specs=(pl.BlockSpec((1, seq, tn), lambda i, j: (i, 0, j)),
                   pl.BlockSpec((1, t, LORA_G), lambda i, j: (i, 0, 0))),
        scratch_shapes=[pltpu.VMEM((seq, d), BF16)],
        compiler_params=pltpu.CompilerParams(dimension_semantics=("parallel", "arbitrary"),
                                             vmem_limit_bytes=VMEM_LIMIT),
        name="inproj_conv",
    )(h_ctx, h_lat, mod, w_conv, conv_w)


def _inproj_pg_kernel(x_ref, mod_ref, wp_ref, wg_ref, pool_ref, gate_ref):
    x = x_ref[0]
    hn = (x * _rms_scale(x) * mod_ref[0, 4:5, :] * (1.0 + mod_ref[0, 1:2, :]) + mod_ref[0, 0:1, :]).astype(BF16)
    pool_ref[0] = jnp.dot(hn, wp_ref[...], preferred_element_type=F32).astype(BF16)
    gate_ref[0] = _sigmoid(jnp.dot(hn, wg_ref[...], preferred_element_type=F32)).astype(BF16)


def _inproj_pg(x, mod, w_pool, w_gate):
    b, t, d = x.shape
    tm = min(512, t)
    assert t % tm == 0
    npool, ngate = w_pool.shape[1], w_gate.shape[1]
    return pl.pallas_call(
        _inproj_pg_kernel,
        out_shape=(jax.ShapeDtypeStruct((b, t, npool), BF16), jax.ShapeDtypeStruct((b, t, ngate), BF16)),
        grid=(b, t // tm),
        in_specs=[pl.BlockSpec((1, tm, d), lambda i, j: (i, j, 0)),
                  pl.BlockSpec((1, 8, d), lambda i, j: (i, 0, 0)),
                  pl.BlockSpec((d, npool), lambda i, j: (0, 0)),
                  pl.BlockSpec((d, ngate), lambda i, j: (0, 0))],
        out_specs=(pl.BlockSpec((1, tm, npool), lambda i, j: (i, j, 0)),
                   pl.BlockSpec((1, tm, ngate), lambda i, j: (i, j, 0))),
        compiler_params=pltpu.CompilerParams(dimension_semantics=("parallel", "parallel"),
                                             vmem_limit_bytes=VMEM_LIMIT),
        name="inproj_pool_gate",
    )(x, mod, w_pool, w_gate)


def _stack(x, bd):
    xb = x.astype(BF16)
    return jnp.concatenate([xb] * HEADS_PER_LANE_GROUP, axis=0) * bd


def _rwkv_kernel(xf_ref, xb_ref, w2_ref, a2_ref, vec_ref, bd_ref, icat_ref, mask_ref, tri_ref,
                 of_ref, ob_ref, s_ref, *, d_rwkv):
    step = pl.program_id(1)

    @pl.when(step == 0)
    def _():
        s_ref[...] = jnp.zeros_like(s_ref)

    bd = bd_ref[...]
    icat = icat_ref[...]
    n_lane_groups = d_rwkv // LANE_GROUP
    o_xw = 3 * d_rwkv
    o_xa = o_xw + 2 * LORA_W

    def bmm(a, b):
        return jnp.dot(a.astype(BF16), _stack(b, bd), preferred_element_type=F32)

    probs = [(d, g) for d in range(2) for g in range(n_lane_groups)]
    x_refs = (xf_ref, xb_ref)
    o_refs = (of_ref, ob_ref)
    C = CHUNK
    icat_b = icat.astype(BF16)

    def lanes(g):
        return slice(g * LANE_GROUP, (g + 1) * LANE_GROUP)

    def each(fn, *lists):
        return [fn(*args) for args in zip(*lists)]

    def split_rows(x, n):
        return [x[i * C:(i + 1) * C] for i in range(n)]

    def shared_bd_dot(blocks):
        return split_rows(_bdot(jnp.concatenate(blocks, axis=0), bd), len(blocks))

    tanh_xw = [jnp.tanh(x_refs[d][0, :, o_xw:o_xw + 2 * LORA_W].astype(F32)).astype(BF16) for d in range(2)]
    xa = [x_refs[d][0, :, o_xa:o_xa + 2 * LORA_A] for d in range(2)]
    r = [x_refs[d][0, :, lanes(g)].astype(F32) for d, g in probs]
    k = [x_refs[d][0, :, d_rwkv + g * LANE_GROUP:d_rwkv + (g + 1) * LANE_GROUP].astype(F32) for d, g in probs]
    v = [x_refs[d][0, :, 2 * d_rwkv + g * LANE_GROUP:2 * d_rwkv + (g + 1) * LANE_GROUP].astype(F32)
         for d, g in probs]

    z = [vec_ref[d:d + 1, lanes(g)] + jnp.dot(tanh_xw[d], w2_ref[d, :, lanes(g)], preferred_element_type=F32)
         for d, g in probs]
    lw = [-jnp.exp(-_softplus(-zi) - DECAY_OFFSET) for zi in z]
    a = [_sigmoid(vec_ref[2 + d:3 + d, lanes(g)]
                  + jnp.dot(xa[d], a2_ref[d, :, lanes(g)], preferred_element_type=F32)) for d, g in probs]
    kkr = [ki * vec_ref[4:5, lanes(g)] for ki, (d, g) in zip(k, probs)]
    kd = [ki * (1.0 + (ai - 1.0) * vec_ref[5:6, lanes(g)]) for ki, ai, (d, g) in zip(k, a, probs)]
    rkr = [ri * kdi * vec_ref[6:7, lanes(g)] for ri, kdi, (d, g) in zip(r, kd, probs)]
    sums = shared_bd_dot([x * x for x in kkr] + rkr)
    kk = [x / jnp.maximum(jnp.sqrt(ss), 1e-12) for x, ss in zip(kkr, sums[:len(probs)])]
    bonus = [bs * vi for bs, vi in zip(sums[len(probs):], v)]

    def cumdecay(lwi, d):
        l0, l1, l2 = _split3(lwi)
        cs = jnp.dot(tri_ref[d], jnp.concatenate([l0, l1, l2], axis=1), preferred_element_type=F32)
        return cs[:, :LANE_GROUP] + cs[:, LANE_GROUP:2 * LANE_GROUP] + cs[:, 2 * LANE_GROUP:]

    cl = [cumdecay(lwi, d) for lwi, (d, g) in zip(lw, probs)]
    e_pos = [jnp.exp(x) for x in cl]
    e_neg = [jnp.exp(-x) for x in cl]
    at = [-kki * jnp.exp(cli - lwi) for kki, cli, lwi in zip(kk, cl, lw)]
    rt = each(lambda ri, e: ri * e, r, e_pos)
    bt = each(lambda kki, ai, e: kki * ai * e, kk, a, e_neg)
    kt = each(lambda kdi, e: kdi * e, kd, e_neg)

    lhs = each(lambda ai, ri: jnp.concatenate([ai, ri, icat], axis=0).astype(BF16), at, rt)
    gb = each(lambda l, x: _bdot_t(l, _stack(x, bd)), lhs, bt)
    gk = each(lambda l, x: _bdot_t(l, _stack(x, bd)), lhs, kt)
    strict = [mask_ref[d, 0] for d, g in probs]
    incl = [mask_ref[d, 1] for d, g in probs]
    a_ab = each(lambda x, m: x[0:C] * m, gb, strict)
    a_rb = each(lambda x, m: x[C:2 * C] * m, gb, incl)
    bt_t = [x[2 * C:3 * C] for x in gb]
    a_ak = each(lambda x, m: x[0:C] * m, gk, strict)
    a_rk = each(lambda x, m: x[C:2 * C] * m, gk, incl)
    kt_t = [x[2 * C:3 * C] for x in gk]

    tinv = [icat + x for x in a_ab]
    lp = a_ab
    for _ in range(5):
        lp = each(bmm, lp, lp)
        tinv = each(lambda t, l: t + bmm(t, l), tinv, lp)

    hv = each(lambda p, q_, t_, vi: jnp.dot(jnp.concatenate([p, q_, t_], axis=0).astype(BF16), _stack(vi, bd),
                                            preferred_element_type=F32), a_ak, a_rk, kt_t, v)
    wt = each(bmm, tinv, at)
    ut = each(lambda t, h: bmm(t, h[0:C]), tinv, hv)
    lhs2 = each(lambda p, q_: jnp.concatenate([p, q_], axis=0).astype(BF16), a_rb, bt_t)
    hw = each(lambda l, x: jnp.dot(l, _stack(x, bd), preferred_element_type=F32), lhs2, wt)
    hu = each(lambda l, x: jnp.dot(l, _stack(x, bd), preferred_element_type=F32), lhs2, ut)

    def gam_rows(e, d):
        last = C - 1 if d == 0 else 0
        gam = e[last:last + 1, :]
        gam_hi = gam.astype(BF16)
        gam_lo = (gam - gam_hi.astype(F32)).astype(BF16)
        return [icat_b * gam_hi, icat_b * gam_lo]

    gparts = split_rows(jnp.dot(jnp.concatenate([blk for e, (d, g) in zip(e_pos, probs) for blk in gam_rows(e, d)],
                                                axis=0), bd, preferred_element_type=F32), 2 * len(probs))
    gcat = [gparts[2 * i] + gparts[2 * i + 1] for i in range(len(probs))]

    s0 = [s_ref[d, g] for d, g in probs]
    hs = each(lambda ri, hwi, s: jnp.dot(jnp.concatenate([ri + hwi[0:C], hwi[C:2 * C]], axis=0).astype(BF16),
                                         _stack(s, bd), preferred_element_type=F32), rt, hw, s0)
    y = each(lambda hsi, hui, hvi: hsi[0:C] + hui[0:C] + hvi[C:2 * C], hs, hu, hv)
    for (d, g), gc, s, hsi, hui, hvi in zip(probs, gcat, s0, hs, hu, hv):
        s_ref[d, g] = gc * (s + hsi[C:2 * C] + hui[C:2 * C] + hvi[2 * C:3 * C])

    inv_n = 1.0 / HEAD_DIM
    mu = [m * inv_n for m in shared_bd_dot(y)]
    yc = each(lambda yi, m: yi - m, y, mu)
    var = [m * inv_n for m in shared_bd_dot([x * x for x in yc])]
    for (d, g), yci, vari, bi in zip(probs, yc, var, bonus):
        out = yci * lax.rsqrt(vari + GN_EPS) * vec_ref[7:8, lanes(g)] + vec_ref[8:9, lanes(g)] + bi
        o_refs[d][0, :, lanes(g)] = out.astype(BF16)


def _rwkv(pconv, w2p, a2p, vecs, n_ctx, d_rwkv):
    b, seq, n = pconv.shape
    nc = seq // CHUNK
    nc_ctx = n_ctx // CHUNK
    assert seq % CHUNK == 0 and n_ctx % CHUNK == 0 and d_rwkv % LANE_GROUP == 0
    n_lane_groups = d_rwkv // LANE_GROUP

    lane = jnp.arange(LANE_GROUP)
    rowi = jnp.arange(CHUNK)
    j = (lane % HEAD_DIM)[None, :]
    t = rowi[:, None]
    bd = (lane[:, None] // HEAD_DIM == lane[None, :] // HEAD_DIM).astype(BF16)
    icat = (j == t).astype(F32)
    masks = jnp.stack([jnp.stack([(j < t), (j <= t)]), jnp.stack([(j > t), (j >= t)])]).astype(F32)
    tri = jnp.stack([rowi[None, :] <= rowi[:, None], rowi[None, :] >= rowi[:, None]]).astype(BF16)

    def bwd_chunk(s):
        return jnp.where(s < nc_ctx, nc_ctx - 1 - s, nc + nc_ctx - 1 - s)

    nc_lat = nc - nc_ctx
    out_f = lambda i, s: (i, jnp.where(s < nc_ctx, nc_lat, s - nc_ctx), 0)
    out_b = lambda i, s: (i, jnp.where(s < nc_ctx, nc_lat, nc - 1 - s), 0)
    out_sds = jax.ShapeDtypeStruct((b, seq - n_ctx + CHUNK, d_rwkv), BF16)

    const2 = lambda i, s: (0, 0)
    const3 = lambda i, s: (0, 0, 0)
    const4 = lambda i, s: (0, 0, 0, 0)
    return pl.pallas_call(
        functools.partial(_rwkv_kernel, d_rwkv=d_rwkv),
        out_shape=(out_sds, out_sds),
        grid=(b, nc),
        in_specs=[pl.BlockSpec((1, CHUNK, n), lambda i, s: (i, s, 0)),
                  pl.BlockSpec((1, CHUNK, n), lambda i, s: (i, bwd_chunk(s), 0)),
                  pl.BlockSpec(w2p.shape, const3),
                  pl.BlockSpec(a2p.shape, const3),
                  pl.BlockSpec(vecs.shape, const2),
                  pl.BlockSpec(bd.shape, const2),
                  pl.BlockSpec(icat.shape, const2),
                  pl.BlockSpec(masks.shape, const4),
                  pl.BlockSpec(tri.shape, const3)],
        out_specs=(pl.BlockSpec((1, CHUNK, d_rwkv), out_f), pl.BlockSpec((1, CHUNK, d_rwkv), out_b)),
        scratch_shapes=[pltpu.VMEM((2, n_lane_groups, CHUNK, LANE_GROUP), F32)],
        compiler_params=pltpu.CompilerParams(dimension_semantics=("parallel", "arbitrary"),
                                             vmem_limit_bytes=VMEM_LIMIT),
        name="rwkv_scan",
    )(pconv, pconv, w2p, a2p, vecs, bd, icat, masks, tri)


def _pool_kernel(u_ref, win_ref, inv_ref, w_ref, sc_ref, o_ref):
    u = u_ref[0]
    mean = jnp.dot(win_ref[0], u, preferred_element_type=F32) * inv_ref[0]
    diff = mean - u.astype(F32)
    o_ref[0] = (_bdot(diff, w_ref[0]) * sc_ref[0]).astype(BF16)


def _pool(pool_in, pool_w, pool_scale):
    b, t, dp = pool_in.shape
    ng = len(POOL_WINDOWS)
    cg = POOL_GROUP_DIM
    assert dp == ng * cg and t % GRID_W == 0
    tok = jnp.arange(t)
    row, col = tok // GRID_W, tok % GRID_W
    wins, invs = [], []
    for win in POOL_WINDOWS:
        lo = win // 2
        hi = win - lo - 1
        dr = row[None, :] - row[:, None]
        dc = col[None, :] - col[:, None]
        m = (dr >= -lo) & (dr <= hi) & (dc >= -lo) & (dc <= hi)
        wins.append(m.astype(BF16))
        invs.append(1.0 / jnp.sum(m, axis=1, dtype=F32))
    wins = jnp.stack(wins)
    invs = jnp.broadcast_to(jnp.stack(invs)[:, :, None], (ng, t, cg))
    return pl.pallas_call(
        _pool_kernel,
        out_shape=jax.ShapeDtypeStruct((b, t, dp), BF16),
        grid=(ng, b),
        in_specs=[pl.BlockSpec((1, t, cg), lambda g, i: (i, 0, g)),
                  pl.BlockSpec((1, t, t), lambda g, i: (g, 0, 0)),
                  pl.BlockSpec((1, t, cg), lambda g, i: (g, 0, 0)),
                  pl.BlockSpec((1, cg, cg), lambda g, i: (g, 0, 0)),
                  pl.BlockSpec((1, 1, cg), lambda g, i: (g, 0, 0))],
        out_specs=pl.BlockSpec((1, t, cg), lambda g, i: (i, 0, g)),
        compiler_params=pltpu.CompilerParams(dimension_semantics=("arbitrary", "arbitrary"),
                                             vmem_limit_bytes=VMEM_LIMIT),
        name="pool_mix",
    )(pool_in, wins, invs, pool_w.astype(BF16), pool_scale.reshape(ng, 1, cg))


def _merge_kernel(x_ref, py_ref, yf_ref, yb_ref, xg_ref, gate_ref, mod_ref, g2_ref, pp_ref, rp_ref, wo_ref, o_ref):
    d = x_ref.shape[2]
    g = jnp.dot(_sigmoid(xg_ref[0].astype(F32)).astype(BF16), g2_ref[...], preferred_element_type=F32)
    ry = (yf_ref[0].astype(F32) + yb_ref[0].astype(F32)) * g
    gates = gate_ref[0]
    m = (gates[:, :d].astype(F32) * jnp.dot(py_ref[0], pp_ref[...], preferred_element_type=F32)
         + gates[:, d:].astype(F32) * _bdot(ry, rp_ref[...]))
    mo = _bdot(m, wo_ref[...])
    o_ref[0] = x_ref[0] + mod_ref[0, 5:6, :] * (mo * _rms_scale(mo) * mod_ref[0, 6:7, :])


def _merge(x, pool_y, yf, yb, xg, gates, mod, g2, pool_proj, rwkv_proj, w_out):
    b, t, d = x.shape
    tm = math.gcd(t, 512)
    dr = yf.shape[2]
    lat = lambda i, j: (i, j, 0)
    const = lambda i, j: (0, 0)
    return pl.pallas_call(
        _merge_kernel,
        out_shape=jax.ShapeDtypeStruct((b, t, d), F32),
        grid=(b, t // tm),
        in_specs=[pl.BlockSpec((1, tm, d), lat),
                  pl.BlockSpec((1, tm, pool_y.shape[2]), lat),
                  pl.BlockSpec((1, tm, dr), lat),
                  pl.BlockSpec((1, tm, dr), lat),
                  pl.BlockSpec((1, tm, LORA_G), lat),
                  pl.BlockSpec((1, tm, gates.shape[2]), lat),
                  pl.BlockSpec((1, 8, d), lambda i, j: (i, 0, 0)),
                  pl.BlockSpec(g2.shape, const),
                  pl.BlockSpec(pool_proj.shape, const),
                  pl.BlockSpec(rwkv_proj.shape, const),
                  pl.BlockSpec(w_out.shape, const)],
        out_specs=pl.BlockSpec((1, tm, d), lat),
        compiler_params=pltpu.CompilerParams(dimension_semantics=("parallel", "parallel"),
                                             vmem_limit_bytes=VMEM_LIMIT),
        name="merge_out",
    )(x, pool_y, yf, yb, xg, gates, mod, g2, pool_proj, rwkv_proj, w_out)


def _moe_kernel(h_ref, mod_ref, wr_ref, br_ref, wg_ref, wu_ref, wd_ref, o_ref, u_ref, gate_ref, slotc_ref, slott_ref,
                *, n_exp):
    g = pl.program_id(1)
    tm = h_ref.shape[0]
    lanes = gate_ref.shape[1]
    n_grp = n_exp // EXPERTS_PER_GROUP
    sb = MOE_SLOT_BLOCK

    @pl.when(g == 0)
    def _():
        h = h_ref[...]
        u = h * _rms_scale(h) * mod_ref[0, 2:3, :] * (1.0 + mod_ref[0, 1:2, :]) + mod_ref[0, 0:1, :]
        u_ref[...] = u.astype(BF16)
        logits = _dot_f32(u, wr_ref[...]) + br_ref[...]
        lane_i = lax.broadcasted_iota(jnp.int32, logits.shape, 1)
        lane = lane_i.astype(F32)
        lane_grp = (lane_i // EXPERTS_PER_GROUP).astype(F32)
        neg = -jnp.inf
        big = float(lanes)
        is_grp = (lane_i >= n_exp) & (lane_i < n_exp + n_grp)
        gl = jnp.where(is_grp, logits, neg)
        gmax = jnp.max(gl, axis=-1, keepdims=True)
        p_group = 1.0 / jnp.sum(jnp.where(is_grp, jnp.exp(gl - gmax), 0.0), axis=-1, keepdims=True)
        g_idx = jnp.min(jnp.where(gl == gmax, lane - n_exp, big), axis=-1, keepdims=True)
        in_grp = (lane_i < n_exp) & (lane_grp == g_idx)
        el = jnp.where(in_grp, logits, neg)
        m1 = jnp.max(el, axis=-1, keepdims=True)
        i1 = jnp.min(jnp.where(el == m1, lane, big), axis=-1, keepdims=True)
        el2 = jnp.where(lane == i1, neg, el)
        m2 = jnp.max(el2, axis=-1, keepdims=True)
        i2 = jnp.min(jnp.where(el2 == m2, lane, big), axis=-1, keepdims=True)
        e21 = jnp.exp(m2 - m1)
        p1 = 1.0 / (1.0 + e21)
        p2 = e21 * p1
        gate_ref[...] = p_group * (jnp.where(lane == i1, p1, 0.0) + jnp.where(lane == i2, p2, 0.0))
        member = jnp.where(lane == g_idx, 1.0, 0.0)
        tri = (lax.broadcasted_iota(jnp.int32, (tm, tm), 1) <= lax.broadcasted_iota(jnp.int32, (tm, tm), 0))
        count = jnp.dot(jnp.where(tri, 1.0, 0.0).astype(BF16), member.astype(BF16), preferred_element_type=F32)
        slot = jnp.where(member > 0.0, count - 1.0, -1.0)
        slotc_ref[...] = slot
        slott_ref[...] = slot.T
        o_ref[...] = jnp.zeros_like(o_ref)

    lane_t = lax.broadcasted_iota(jnp.int32, (tm, lanes), 1)
    slot_col = jnp.sum(jnp.where(lane_t == g, slotc_ref[...], 0.0), axis=-1, keepdims=True)
    slot_cb = jnp.broadcast_to(slot_col, (tm, sb))
    slot_row = slott_ref[pl.ds(g, 1), :]
    n_blk = ((jnp.max(slot_col, axis=0, keepdims=True) + float(sb)) * (1.0 / sb)).astype(jnp.int32)[0, 0]
    gates = gate_ref[...]
    gates_hi = gates.astype(BF16)
    gates_lo = (gates - gates_hi.astype(F32)).astype(BF16)
    s_iota = lax.broadcasted_iota(jnp.int32, (sb, tm), 0).astype(F32)
    l_iota = lax.broadcasted_iota(jnp.int32, (tm, sb), 1).astype(F32)
    lane_s = lax.broadcasted_iota(jnp.int32, (sb, lanes), 1)

    def block(blk, carry):
        base = (blk * sb).astype(F32)
        sel = jnp.where(slot_row == s_iota + base, 1.0, 0.0).astype(BF16)
        x = jnp.dot(sel, u_ref[...], preferred_element_type=F32).astype(BF16)
        gs = (jnp.dot(sel, gates_hi, preferred_element_type=F32)
              + jnp.dot(sel, gates_lo, preferred_element_type=F32))
        hs = []
        for e in range(EXPERTS_PER_GROUP):
            ge = jnp.sum(jnp.where(lane_s == g * EXPERTS_PER_GROUP + e, gs, 0.0), axis=-1, keepdims=True)
            hg = jnp.dot(x, wg_ref[e], preferred_element_type=F32)
            hu = jnp.dot(x, wu_ref[e], preferred_element_type=F32)
            hs.append((hg * _sigmoid(hg) * hu * ge).astype(BF16))
        wd = wd_ref[...]
        y = jnp.dot(jnp.concatenate(hs, axis=1), wd.reshape(wd.shape[0] * wd.shape[1], wd.shape[2]),
                    preferred_element_type=F32).astype(BF16)
        sel_t = jnp.where(slot_cb == l_iota + base, 1.0, 0.0).astype(BF16)
        o_ref[...] += jnp.dot(sel_t, y, preferred_element_type=F32)
        return carry

    lax.fori_loop(0, n_blk, block, 0)

    @pl.when(g == pl.num_programs(1) - 1)
    def _():
        f = o_ref[...]
        o_ref[...] = h_ref[...] + mod_ref[0, 3:4, :] * (f * _rms_scale(f) * mod_ref[0, 4:5, :])


def _moe(h1, mod, w_router, b_router, w_gate, w_up, w_down, tokens_per_batch):
    n, d = h1.shape
    n_exp, _, d_exp = w_gate.shape
    tm = min(1024, tokens_per_batch)
    exp_blk = EXPERTS_PER_GROUP
    assert n % tm == 0 and tokens_per_batch % tm == 0 and n_exp % exp_blk == 0 and tm % MOE_SLOT_BLOCK == 0
    per_b = tokens_per_batch // tm
    lanes = w_router.shape[1]
    return pl.pallas_call(
        functools.partial(_moe_kernel, n_exp=n_exp),
        out_shape=jax.ShapeDtypeStruct((n, d), F32),
        grid=(n // tm, n_exp // exp_blk),
        in_specs=[pl.BlockSpec((tm, d), lambda i, j: (i, 0)),
                  pl.BlockSpec((1, 8, d), lambda i, j: (i // per_b, 0, 0)),
                  pl.BlockSpec((d, lanes), lambda i, j: (0, 0)),
                  pl.BlockSpec((1, lanes), lambda i, j: (0, 0)),
                  pl.BlockSpec((exp_blk, d, d_exp), lambda i, j: (j, 0, 0)),
                  pl.BlockSpec((exp_blk, d, d_exp), lambda i, j: (j, 0, 0)),
                  pl.BlockSpec((exp_blk, d_exp, d), lambda i, j: (j, 0, 0))],
        out_specs=pl.BlockSpec((tm, d), lambda i, j: (i, 0)),
        scratch_shapes=[pltpu.VMEM((tm, d), BF16), pltpu.VMEM((tm, lanes), F32), pltpu.VMEM((tm, lanes), F32),
                        pltpu.VMEM((lanes, tm), F32)],
        compiler_params=pltpu.CompilerParams(dimension_semantics=("parallel", "arbitrary"),
                                             vmem_limit_bytes=VMEM_LIMIT),
        name="moe",
    )(h1, mod, w_router, b_router, w_gate, w_up, w_down)


def _pad_rows(a, rows):
    return jnp.concatenate([a, jnp.zeros((rows - a.shape[0],) + a.shape[1:], a.dtype)], axis=0)


def _layer(l, h_lat, h_ctx, c, c_ctx, ada_w, ada_b, norm_gains, w_in, conv_w, pool_w, pool_scale, pool_proj,
           w0, w2, a0, a2, g2, k_k, k_a, r_k, lnx_w, lnx_b, rwkv_proj, w_out, router_group_w, router_group_b,
           router_expert_w, router_expert_b, expert_w_gate, expert_w_up, expert_w_down):
    b, t, d = h_lat.shape
    n_ctx = h_ctx.shape[1]
    d_pool = pool_scale.shape[1]
    d_rwkv = k_k.shape[1]
    d_conv = conv_w.shape[2]
    n_exp = expert_w_gate.shape[1]

    cc = _pad_rows(jnp.concatenate([c, c_ctx[None, :]], axis=0), -(-(b + 1) // 8) * 8)
    ada = _ada(cc, ada_w[l], ada_b[l][None, :])
    sh1, sc1, gt1, sh2, sc2, gt2 = [ada[:b, i * d:(i + 1) * d] for i in range(6)]
    csh1 = jnp.broadcast_to(ada[b, 0:d], (b, d))
    csc1 = jnp.broadcast_to(ada[b, d:2 * d], (b, d))
    gains = [jnp.broadcast_to(norm_gains[l, i], (b, d)) for i in range(4)]
    zero = jnp.zeros((b, d), F32)
    mod1 = jnp.stack([sh1, sc1, csh1, csc1, gains[0], gt1, gains[1], zero], axis=1)
    mod2 = jnp.stack([sh2, sc2, gains[2], gt2, gains[3], zero, zero, zero], axis=1)

    w_in_b = w_in[l].astype(BF16)
    w_pool = w_in_b[:, :d_pool]
    w_conv = w_in_b[:, d_pool:d_pool + d_conv]
    w_gate = w_in_b[:, d_pool + d_conv:]

    col_pad = -d_conv % 512
    pconv, xg = _inproj_conv(h_ctx, h_lat, mod1, jnp.pad(w_conv, ((0, 0), (0, col_pad))),
                             jnp.pad(conv_w[l], ((0, 8 - conv_w.shape[1]), (0, col_pad))), d_conv - LORA_G)
    pool_in, gates = _inproj_pg(h_lat, mod1, w_pool, w_gate)

    zw = jnp.zeros_like(w2[l, 0])
    za = jnp.zeros_like(a2[l, 0])
    w2p = jnp.stack([jnp.concatenate([w2[l, 0], zw], axis=0), jnp.concatenate([zw, w2[l, 1]], axis=0)]).astype(BF16)
    a2p = jnp.stack([jnp.concatenate([a2[l, 0], za], axis=0), jnp.concatenate([za, a2[l, 1]], axis=0)]).astype(BF16)
    vecs = _pad_rows(jnp.stack([w0[l, 0], w0[l, 1], a0[l, 0], a0[l, 1], k_k[l], k_a[l], r_k[l].reshape(-1),
                                lnx_w[l], lnx_b[l]]), 16)
    yf, yb = _rwkv(pconv, w2p, a2p, vecs, n_ctx, d_rwkv)

    pool_y = _pool(pool_in, pool_w[l], pool_scale[l])
    h1 = _merge(h_lat, pool_y, yf, yb, xg, gates, mod1, g2[l].astype(BF16), pool_proj[l].astype(BF16),
                rwkv_proj[l].astype(BF16), w_out[l].astype(BF16))

    lanes = 128
    w_router = jnp.concatenate([router_expert_w[l], router_group_w[l],
                                jnp.zeros((d, lanes - n_exp - N_GROUPS), F32)], axis=1)
    b_router = jnp.concatenate([router_expert_b[l], router_group_b[l],
                                jnp.zeros((lanes - n_exp - N_GROUPS,), F32)])[None, :]
    h2 = _moe(h1.reshape(b * t, d), mod2, w_router, b_router, expert_w_gate[l].astype(BF16),
              expert_w_up[l].astype(BF16), expert_w_down[l].astype(BF16), t)
    return h2.reshape(b, t, d)


def kernel(x, c, ctx, c_ctx, ada_w, ada_b, norm_gains, w_in, conv_w, pool_w, pool_scale, pool_proj, w0, w2, a0, a2, g2, k_k, k_a, r_k, lnx_w, lnx_b, rwkv_proj, w_out, router_group_w, router_group_b, router_expert_w, router_expert_b, expert_w_gate, expert_w_up, expert_w_down):
    depth = ada_w.shape[0]
    assert depth == 1, "the context stream update between layers is not implemented"
    h_lat = x.astype(F32)
    h_ctx = ctx.astype(F32)
    h_lat = _layer(0, h_lat, h_ctx, c, c_ctx, ada_w, ada_b, norm_gains, w_in, conv_w, pool_w, pool_scale,
                   pool_proj, w0, w2, a0, a2, g2, k_k, k_a, r_k, lnx_w, lnx_b, rwkv_proj, w_out,
                   router_group_w, router_group_b, router_expert_w, router_expert_b,
                   expert_w_gate, expert_w_up, expert_w_down)
    return h_lat.astype(x.dtype)
```

```python
import functools
import math

import jax
import jax.numpy as jnp
from jax import lax
from jax.experimental import pallas as pl
from jax.experimental.pallas import tpu as pltpu

F32 = jnp.float32
BF16 = jnp.bfloat16

GRID_W = 64
POOL_WINDOWS = (2, 4, 8, 16)
POOL_GROUP_DIM = 128
HEAD_DIM = 64
LORA_W = 64
LORA_A = 64
LORA_G = 128
N_GROUPS = 4
EXPERTS_PER_GROUP = 8
GN_EPS = 64e-5
RMS_EPS = 1e-6
DECAY_OFFSET = 0.5

CHUNK = 64
HEADS_PER_LANE_GROUP = 4
LANE_GROUP = HEADS_PER_LANE_GROUP * HEAD_DIM
MOE_SLOT_BLOCK = 128
VMEM_LIMIT = 56 * 1024 * 1024


def _bdot(a, b):
    return jnp.dot(a.astype(BF16), b.astype(BF16), preferred_element_type=F32)


def _bdot_t(a, b):
    return lax.dot_general(a.astype(BF16), b.astype(BF16), (((1,), (1,)), ((), ())),
                           preferred_element_type=F32)


def _split3(a):
    hi = a.astype(BF16)
    r1 = a - hi.astype(F32)
    mid = r1.astype(BF16)
    lo = (r1 - mid.astype(F32)).astype(BF16)
    return hi, mid, lo


def _dot_f32(a, b):
    a0, a1, a2 = _split3(a)
    b0, b1, b2 = _split3(b)
    d = lambda p, q: jnp.dot(p, q, preferred_element_type=F32)
    return (d(a0, b0) + (d(a0, b1) + d(a1, b0))
            + (d(a0, b2) + d(a1, b1) + d(a2, b0)))


def _dot_hilo(a, b):
    a0 = a.astype(BF16)
    a1 = (a - a0.astype(F32)).astype(BF16)
    b0 = b.astype(BF16)
    b1 = (b - b0.astype(F32)).astype(BF16)
    d = lambda p, q: jnp.dot(p, q, preferred_element_type=F32)
    return d(a0, b0) + (d(a0, b1) + d(a1, b0))


def _sigmoid(x):
    return 1.0 / (1.0 + jnp.exp(-x))


def _rms_scale(x):
    return lax.rsqrt(jnp.mean(x * x, axis=-1, keepdims=True) + RMS_EPS)


def _ada_kernel(c_ref, w_ref, b_ref, o_ref):
    c = c_ref[...]
    o_ref[...] = _dot_f32(c * _sigmoid(c), w_ref[...]) + b_ref[...]


def _ada(cc, ada_w, ada_b):
    rows, d = cc.shape
    n = ada_w.shape[1]
    tn = 1024
    return pl.pallas_call(
        _ada_kernel,
        out_shape=jax.ShapeDtypeStruct((rows, n), F32),
        grid=(n // tn,),
        in_specs=[pl.BlockSpec((rows, d), lambda j: (0, 0)),
                  pl.BlockSpec((d, tn), lambda j: (0, j)),
                  pl.BlockSpec((1, tn), lambda j: (0, j))],
        out_specs=pl.BlockSpec((rows, tn), lambda j: (0, j)),
        compiler_params=pltpu.CompilerParams(dimension_semantics=("arbitrary",),
                                             vmem_limit_bytes=VMEM_LIMIT),
        name="ada",
    )(cc, ada_w, ada_b)


def _inproj_conv_kernel(c_ref, x_ref, mod_ref, w_ref, cw_ref, o_ref, xg_ref, hn_ref, *, row_blk, lat_blk, xg_col):
    n_ctx = c_ref.shape[1]
    seq = n_ctx + x_ref.shape[1]

    @pl.when(pl.program_id(1) == 0)
    def _():
        gain = mod_ref[0, 4:5, :]
        for r0 in range(0, seq, row_blk):
            is_ctx = r0 < n_ctx
            sh = mod_ref[0, 2:3, :] if is_ctx else mod_ref[0, 0:1, :]
            sc = mod_ref[0, 3:4, :] if is_ctx else mod_ref[0, 1:2, :]
            x = c_ref[0, r0:r0 + row_blk, :] if is_ctx else x_ref[0, r0 - n_ctx:r0 - n_ctx + row_blk, :]
            hn = x * _rms_scale(x) * gain * (1.0 + sc) + sh
            hn_ref[r0:r0 + row_blk, :] = hn.astype(BF16)

    def conv(q, first_row, masked):
        n = q.shape[0]
        prev = pltpu.roll(q, 1, 0)
        nxt = pltpu.roll(q, n - 1, 0)
        if masked:
            row = lax.broadcasted_iota(jnp.int32, q.shape, 0) + first_row
            prev = jnp.where((row == 0) | (row == n_ctx), 0.0, prev)
            nxt = jnp.where((row == n_ctx - 1) | (row == seq - 1), 0.0, nxt)
        return (cw_ref[0:1, :] * prev + cw_ref[1:2, :] * q + cw_ref[2:3, :] * nxt).astype(BF16)

    halo = 16
    blocks = [(0, n_ctx)] + [(r, r + lat_blk) for r in range(n_ctx, seq, lat_blk)]
    for r0, r1 in blocks:
        at_start = r0 in (0, n_ctx)
        at_end = r1 in (n_ctx, seq)
        w0 = r0 if at_start else r0 - halo
        w1 = r1 if at_end else r1 + halo
        q = jnp.dot(hn_ref[w0:w1, :], w_ref[...], preferred_element_type=F32)
        o_ref[0, r0:r1, :] = conv(q, w0, False)[r0 - w0:r1 - w0]
        if at_start:
            o_ref[0, r0:r0 + halo, :] = conv(q[0:2 * halo], w0, True)[0:halo]
        if at_end:
            o_ref[0, r1 - halo:r1, :] = conv(q[w1 - w0 - 2 * halo:], w1 - 2 * halo, True)[halo:]

    tn = o_ref.shape[2]

    @pl.when(pl.program_id(1) == xg_col // tn)
    def _():
        xg_ref[0] = o_ref[0, n_ctx:, xg_col % tn:xg_col % tn + LORA_G]


def _inproj_conv(h_ctx, h_lat, mod, w_conv, conv_w, xg_col):
    b, t, d = h_lat.shape
    n_ctx = h_ctx.shape[1]
    seq = n_ctx + t
    n = w_conv.shape[1]
    tn = 512
    row_blk = math.gcd(math.gcd(t, n_ctx), 256)
    lat_blk = math.gcd(t, 512)
    assert n % tn == 0 and xg_col % LORA_G == 0 and min(lat_blk, n_ctx) >= 32 and n_ctx % 16 == 0
    return pl.pallas_call(
        functools.partial(_inproj_conv_kernel, row_blk=row_blk, lat_blk=lat_blk, xg_col=xg_col),
        out_shape=(jax.ShapeDtypeStruct((b, seq, n), BF16), jax.ShapeDtypeStruct((b, t, LORA_G), BF16)),
        grid=(b, n // tn),
        in_specs=[pl.BlockSpec((1, n_ctx, d), lambda i, j: (i, 0, 0)),
                  pl.BlockSpec((1, t, d), lambda i, j: (i, 0, 0)),
                  pl.BlockSpec((1, 8, d), lambda i, j: (i, 0, 0)),
                  pl.BlockSpec((d, tn), lambda i, j: (0, j)),
                  pl.BlockSpec((8, tn), lambda i, j: (0, j))],
        out_specs=(pl.BlockSpec((1, seq, tn), lambda i, j: (i, 0, j)),
                   pl.BlockSpec((1, t, LORA_G), lambda i, j: (i, 0, 0))),
        scratch_shapes=[pltpu.VMEM((seq, d), BF16)],
        compiler_params=pltpu.CompilerParams(dimension_semantics=("parallel", "arbitrary"),
                                             vmem_limit_bytes=VMEM_LIMIT),
        name="inproj_conv",
    )(h_ctx, h_lat, mod, w_conv, conv_w)


def _inproj_pg_kernel(x_ref, mod_ref, wp_ref, wg_ref, pool_ref, gate_ref):
    x = x_ref[0]
    hn = (x * _rms_scale(x) * mod_ref[0, 4:5, :] * (1.0 + mod_ref[0, 1:2, :]) + mod_ref[0, 0:1, :]).astype(BF16)
    pool_ref[0] = jnp.dot(hn, wp_ref[...], preferred_element_type=F32).astype(BF16)
    gate_ref[0] = _sigmoid(jnp.dot(hn, wg_ref[...], preferred_element_type=F32)).astype(BF16)


def _inproj_pg(x, mod, w_pool, w_gate):
    b, t, d = x.shape
    tm = min(512, t)
    assert t % tm == 0
    npool, ngate = w_pool.shape[1], w_gate.shape[1]
    return pl.pallas_call(
        _inproj_pg_kernel,
        out_shape=(jax.ShapeDtypeStruct((b, t, npool), BF16), jax.ShapeDtypeStruct((b, t, ngate), BF16)),
        grid=(b, t // tm),
        in_specs=[pl.BlockSpec((1, tm, d), lambda i, j: (i, j, 0)),
                  pl.BlockSpec((1, 8, d), lambda i, j: (i, 0, 0)),
                  pl.BlockSpec((d, npool), lambda i, j: (0, 0)),
                  pl.BlockSpec((d, ngate), lambda i, j: (0, 0))],
        out_specs=(pl.BlockSpec((1, tm, npool), lambda i, j: (i, j, 0)),
                   pl.BlockSpec((1, tm, ngate), lambda i, j: (i, j, 0))),
        compiler_params=pltpu.CompilerParams(dimension_semantics=("parallel", "parallel"),
                                             vmem_limit_bytes=VMEM_LIMIT),
        name="inproj_pool_gate",
    )(x, mod, w_pool, w_gate)


def _stack(x, bd):
    xb = x.astype(BF16)
    return jnp.concatenate([xb] * HEADS_PER_LANE_GROUP, axis=0) * bd


def _rwkv_kernel(xf_ref, xb_ref, w2_ref, a2_ref, vec_ref, bd_ref, icat_ref, mask_ref, tri_ref,
                 of_ref, ob_ref, s_ref, *, d_rwkv):
    step = pl.program_id(1)

    @pl.when(step == 0)
    def _():
        s_ref[...] = jnp.zeros_like(s_ref)

    bd = bd_ref[...]
    icat = icat_ref[...]
    n_lane_groups = d_rwkv // LANE_GROUP
    o_xw = 3 * d_rwkv
    o_xa = o_xw + 2 * LORA_W

    def bmm(a, b):
        return jnp.dot(a.astype(BF16), _stack(b, bd), preferred_element_type=F32)

    probs = [(d, g) for d in range(2) for g in range(n_lane_groups)]
    x_refs = (xf_ref, xb_ref)
    o_refs = (of_ref, ob_ref)
    C = CHUNK
    icat_b = icat.astype(BF16)

    def lanes(g):
        return slice(g * LANE_GROUP, (g + 1) * LANE_GROUP)

    def each(fn, *lists):
        return [fn(*args) for args in zip(*lists)]

    def split_rows(x, n):
        return [x[i * C:(i + 1) * C] for i in range(n)]

    def shared_bd_dot(blocks):
        return split_rows(_bdot(jnp.concatenate(blocks, axis=0), bd), len(blocks))

    tanh_xw = [jnp.tanh(x_refs[d][0, :, o_xw:o_xw + 2 * LORA_W].astype(F32)).astype(BF16) for d in range(2)]
    xa = [x_refs[d][0, :, o_xa:o_xa + 2 * LORA_A] for d in range(2)]
    r = [x_refs[d][0, :, lanes(g)].astype(F32) for d, g in probs]
    k = [x_refs[d][0, :, d_rwkv + g * LANE_GROUP:d_rwkv + (g + 1) * LANE_GROUP].astype(F32) for d, g in probs]
    v = [x_refs[d][0, :, 2 * d_rwkv + g * LANE_GROUP:2 * d_rwkv + (g + 1) * LANE_GROUP].astype(F32)
         for d, g in probs]

    z = [vec_ref[d:d + 1, lanes(g)] + jnp.dot(tanh_xw[d], w2_ref[d, :, lanes(g)], preferred_element_type=F32)
         for d, g in probs]
    lw = [-math.exp(-DECAY_OFFSET) * _sigmoid(zi) for zi in z]
    a = [_sigmoid(vec_ref[2 + d:3 + d, lanes(g)]
                  + jnp.dot(xa[d], a2_ref[d, :, lanes(g)], preferred_element_type=F32)) for d, g in probs]
    kkr = [ki * vec_ref[4:5, lanes(g)] for ki, (d, g) in zip(k, probs)]
    kd = [ki * (1.0 + (ai - 1.0) * vec_ref[5:6, lanes(g)]) for ki, ai, (d, g) in zip(k, a, probs)]
    rkr = [ri * kdi * vec_ref[6:7, lanes(g)] for ri, kdi, (d, g) in zip(r, kd, probs)]
    sums = shared_bd_dot([x * x for x in kkr] + rkr)
    kk = [x * lax.rsqrt(jnp.maximum(ss, 1e-24)) for x, ss in zip(kkr, sums[:len(probs)])]
    bonus = [bs * vi for bs, vi in zip(sums[len(probs):], v)]

    def cumdecay(lwi, d):
        l0 = lwi.astype(BF16)
        l1 = (lwi - l0.astype(F32)).astype(BF16)
        cs = jnp.dot(tri_ref[d], jnp.concatenate([l0, l1], axis=1), preferred_element_type=F32)
        return cs[:, :LANE_GROUP] + cs[:, LANE_GROUP:]

    cl = [cumdecay(lwi, d) for lwi, (d, g) in zip(lw, probs)]
    e_pos = [jnp.exp(x) for x in cl]
    e_neg = [jnp.exp(-x) for x in cl]
    at = [-kki * jnp.exp(cli - lwi) for kki, cli, lwi in zip(kk, cl, lw)]
    rt = each(lambda ri, e: ri * e, r, e_pos)
    bt = each(lambda kki, ai, e: kki * ai * e, kk, a, e_neg)
    kt = each(lambda kdi, e: kdi * e, kd, e_neg)

    lhs = each(lambda ai, ri: jnp.concatenate([ai, ri, icat], axis=0).astype(BF16), at, rt)
    gb = each(lambda l, x: _bdot_t(l, _stack(x, bd)), lhs, bt)
    gk = each(lambda l, x: _bdot_t(l, _stack(x, bd)), lhs, kt)
    strict = [mask_ref[d, 0] for d, g in probs]
    incl = [mask_ref[d, 1] for d, g in probs]
    a_ab = each(lambda x, m: x[0:C] * m, gb, strict)
    a_rb = each(lambda x, m: x[C:2 * C] * m, gb, incl)
    bt_t = [x[2 * C:3 * C] for x in gb]
    a_ak = each(lambda x, m: x[0:C] * m, gk, strict)
    a_rk = each(lambda x, m: x[C:2 * C] * m, gk, incl)
    kt_t = [x[2 * C:3 * C] for x in gk]

    tinv = [icat + x for x in a_ab]
    lp = each(bmm, a_ab, a_ab)
    for _ in range(4):
        both = each(lambda l, t: jnp.dot(jnp.concatenate([l, t], axis=0).astype(BF16), _stack(l, bd),
                                         preferred_element_type=F32), lp, tinv)
        lp = [x[0:C] for x in both]
        tinv = each(lambda t, x: t + x[C:2 * C], tinv, both)
    tinv = each(lambda t, l: t + bmm(t, l), tinv, lp)

    hv = each(lambda p, q_, t_, vi: jnp.dot(jnp.concatenate([p, q_, t_], axis=0).astype(BF16), _stack(vi, bd),
                                            preferred_element_type=F32), a_ak, a_rk, kt_t, v)
    lhs2 = each(lambda p, q_, t: jnp.dot(jnp.concatenate([p, q_], axis=0).astype(BF16), _stack(t, bd),
                                         preferred_element_type=F32).astype(BF16), a_rb, bt_t, tinv)
    hw = each(lambda l, x: jnp.dot(l, _stack(x, bd), preferred_element_type=F32), lhs2, at)
    hu = each(lambda l, h: jnp.dot(l, _stack(h[0:C], bd), preferred_element_type=F32), lhs2, hv)

    def gam_rows(e, d):
        last = C - 1 if d == 0 else 0
        gam = e[last:last + 1, :]
        gam_hi = gam.astype(BF16)
        gam_lo = (gam - gam_hi.astype(F32)).astype(BF16)
        return [icat_b * gam_hi, icat_b * gam_lo]

    gparts = split_rows(jnp.dot(jnp.concatenate([blk for e, (d, g) in zip(e_pos, probs) for blk in gam_rows(e, d)],
                                                axis=0), bd, preferred_element_type=F32), 2 * len(probs))
    gcat = [gparts[2 * i] + gparts[2 * i + 1] for i in range(len(probs))]

    s0 = [s_ref[d, g] for d, g in probs]
    hs = each(lambda ri, hwi, s: jnp.dot(jnp.concatenate([ri + hwi[0:C], hwi[C:2 * C]], axis=0).astype(BF16),
                                         _stack(s, bd), preferred_element_type=F32), rt, hw, s0)
    y = each(lambda hsi, hui, hvi: hsi[0:C] + hui[0:C] + hvi[C:2 * C], hs, hu, hv)
    for (d, g), gc, s, hsi, hui, hvi in zip(probs, gcat, s0, hs, hu, hv):
        s_ref[d, g] = gc * (s + hsi[C:2 * C] + hui[C:2 * C] + hvi[2 * C:3 * C])

    inv_n = 1.0 / HEAD_DIM
    mu = [m * inv_n for m in shared_bd_dot(y)]
    yc = each(lambda yi, m: yi - m, y, mu)
    var = [m * inv_n for m in shared_bd_dot([x * x for x in yc])]
    for (d, g), yci, vari, bi in zip(probs, yc, var, bonus):
        out = yci * lax.rsqrt(vari + GN_EPS) * vec_ref[7:8, lanes(g)] + vec_ref[8:9, lanes(g)] + bi
        o_refs[d][0, :, lanes(g)] = out.astype(BF16)


def _rwkv(pconv, w2p, a2p, vecs, n_ctx, d_rwkv):
    b, seq, n = pconv.shape
    nc = seq // CHUNK
    nc_ctx = n_ctx // CHUNK
    assert seq % CHUNK == 0 and n_ctx % CHUNK == 0 and d_rwkv % LANE_GROUP == 0
    n_lane_groups = d_rwkv // LANE_GROUP

    lane = jnp.arange(LANE_GROUP)
    rowi = jnp.arange(CHUNK)
    j = (lane % HEAD_DIM)[None, :]
    t = rowi[:, None]
    bd = (lane[:, None] // HEAD_DIM == lane[None, :] // HEAD_DIM).astype(BF16)
    icat = (j == t).astype(F32)
    masks = jnp.stack([jnp.stack([(j < t), (j <= t)]), jnp.stack([(j > t), (j >= t)])]).astype(F32)
    tri = jnp.stack([rowi[None, :] <= rowi[:, None], rowi[None, :] >= rowi[:, None]]).astype(BF16)

    def bwd_chunk(s):
        return jnp.where(s < nc_ctx, nc_ctx - 1 - s, nc + nc_ctx - 1 - s)

    nc_lat = nc - nc_ctx
    out_f = lambda i, s: (i, jnp.where(s < nc_ctx, nc_lat, s - nc_ctx), 0)
    out_b = lambda i, s: (i, jnp.where(s < nc_ctx, nc_lat, nc - 1 - s), 0)
    out_sds = jax.ShapeDtypeStruct((b, seq - n_ctx + CHUNK, d_rwkv), BF16)

    const2 = lambda i, s: (0, 0)
    const3 = lambda i, s: (0, 0, 0)
    const4 = lambda i, s: (0, 0, 0, 0)
    return pl.pallas_call(
        functools.partial(_rwkv_kernel, d_rwkv=d_rwkv),
        out_shape=(out_sds, out_sds),
        grid=(b, nc),
        in_specs=[pl.BlockSpec((1, CHUNK, n), lambda i, s: (i, s, 0)),
                  pl.BlockSpec((1, CHUNK, n), lambda i, s: (i, bwd_chunk(s), 0)),
                  pl.BlockSpec(w2p.shape, const3),
                  pl.BlockSpec(a2p.shape, const3),
                  pl.BlockSpec(vecs.shape, const2),
                  pl.BlockSpec(bd.shape, const2),
                  pl.BlockSpec(icat.shape, const2),
                  pl.BlockSpec(masks.shape, const4),
                  pl.BlockSpec(tri.shape, const3)],
        out_specs=(pl.BlockSpec((1, CHUNK, d_rwkv), out_f), pl.BlockSpec((1, CHUNK, d_rwkv), out_b)),
        scratch_shapes=[pltpu.VMEM((2, n_lane_groups, CHUNK, LANE_GROUP), F32)],
        compiler_params=pltpu.CompilerParams(dimension_semantics=("parallel", "arbitrary"),
                                             vmem_limit_bytes=VMEM_LIMIT),
        name="rwkv_scan",
    )(pconv, pconv, w2p, a2p, vecs, bd, icat, masks, tri)


def _pool_kernel(u_ref, win_ref, inv_ref, w_ref, sc_ref, o_ref):
    nb = u_ref.shape[0]
    cg = u_ref.shape[2]
    u = jnp.concatenate([u_ref[i] for i in range(nb)], axis=1)
    total = jnp.dot(win_ref[0], u, preferred_element_type=F32)
    for i in range(nb):
        diff = total[:, i * cg:(i + 1) * cg] * inv_ref[0] - u_ref[i].astype(F32)
        o_ref[i] = (_bdot(diff, w_ref[0]) * sc_ref[0]).astype(BF16)


def _pool(pool_in, pool_w, pool_scale):
    b, t, dp = pool_in.shape
    ng = len(POOL_WINDOWS)
    cg = POOL_GROUP_DIM
    assert dp == ng * cg and t % GRID_W == 0
    tok = jnp.arange(t)
    row, col = tok // GRID_W, tok % GRID_W
    wins, invs = [], []
    for win in POOL_WINDOWS:
        lo = win // 2
        hi = win - lo - 1
        dr = row[None, :] - row[:, None]
        dc = col[None, :] - col[:, None]
        m = (dr >= -lo) & (dr <= hi) & (dc >= -lo) & (dc <= hi)
        wins.append(m.astype(BF16))
        invs.append(1.0 / jnp.sum(m, axis=1, dtype=F32))
    wins = jnp.stack(wins)
    invs = jnp.broadcast_to(jnp.stack(invs)[:, :, None], (ng, t, cg))
    nb = 2 if b % 2 == 0 else 1
    return pl.pallas_call(
        _pool_kernel,
        out_shape=jax.ShapeDtypeStruct((b, t, dp), BF16),
        grid=(ng, b // nb),
        in_specs=[pl.BlockSpec((nb, t, cg), lambda g, i: (i, 0, g)),
                  pl.BlockSpec((1, t, t), lambda g, i: (g, 0, 0)),
                  pl.BlockSpec((1, t, cg), lambda g, i: (g, 0, 0)),
                  pl.BlockSpec((1, cg, cg), lambda g, i: (g, 0, 0)),
                  pl.BlockSpec((1, 1, cg), lambda g, i: (g, 0, 0))],
        out_specs=pl.BlockSpec((nb, t, cg), lambda g, i: (i, 0, g)),
        compiler_params=pltpu.CompilerParams(dimension_semantics=("arbitrary", "arbitrary"),
                                             vmem_limit_bytes=VMEM_LIMIT),
        name="pool_mix",
    )(pool_in, wins, invs, pool_w.astype(BF16), pool_scale.reshape(ng, 1, cg))


def _merge_kernel(x_ref, py_ref, yf_ref, yb_ref, xg_ref, gate_ref, mod_ref, g2_ref, pp_ref, rp_ref, wo_ref, o_ref):
    d = x_ref.shape[2]
    g = jnp.dot(_sigmoid(xg_ref[0].astype(F32)).astype(BF16), g2_ref[...], preferred_element_type=F32)
    ry = (yf_ref[0].astype(F32) + yb_ref[0].astype(F32)) * g
    gates = gate_ref[0]
    m = (gates[:, :d].astype(F32) * jnp.dot(py_ref[0], pp_ref[...], preferred_element_type=F32)
         + gates[:, d:].astype(F32) * _bdot(ry, rp_ref[...]))
    mo = _bdot(m, wo_ref[...])
    o_ref[0] = x_ref[0] + mod_ref[0, 5:6, :] * (mo * _rms_scale(mo) * mod_ref[0, 6:7, :])


def _merge(x, pool_y, yf, yb, xg, gates, mod, g2, pool_proj, rwkv_proj, w_out):
    b, t, d = x.shape
    tm = math.gcd(t, 512)
    dr = yf.shape[2]
    lat = lambda i, j: (i, j, 0)
    const = lambda i, j: (0, 0)
    return pl.pallas_call(
        _merge_kernel,
        out_shape=jax.ShapeDtypeStruct((b, t, d), F32),
        grid=(b, t // tm),
        in_specs=[pl.BlockSpec((1, tm, d), lat),
                  pl.BlockSpec((1, tm, pool_y.shape[2]), lat),
                  pl.BlockSpec((1, tm, dr), lat),
                  pl.BlockSpec((1, tm, dr), lat),
                  pl.BlockSpec((1, tm, LORA_G), lat),
                  pl.BlockSpec((1, tm, gates.shape[2]), lat),
                  pl.BlockSpec((1, 8, d), lambda i, j: (i, 0, 0)),
                  pl.BlockSpec(g2.shape, const),
                  pl.BlockSpec(pool_proj.shape, const),
                  pl.BlockSpec(rwkv_proj.shape, const),
                  pl.BlockSpec(w_out.shape, const)],
        out_specs=pl.BlockSpec((1, tm, d), lat),
        compiler_params=pltpu.CompilerParams(dimension_semantics=("parallel", "parallel"),
                                             vmem_limit_bytes=VMEM_LIMIT),
        name="merge_out",
    )(x, pool_y, yf, yb, xg, gates, mod, g2, pool_proj, rwkv_proj, w_out)


def _moe_kernel(h_ref, mod_ref, wr_ref, br_ref, wg_ref, wu_ref, wd_ref, o_ref, u_ref, gate_ref, slotc_ref, slott_ref,
                *, n_exp):
    g = pl.program_id(1)
    tm = h_ref.shape[0]
    lanes = gate_ref.shape[1]
    n_grp = n_exp // EXPERTS_PER_GROUP
    sb = MOE_SLOT_BLOCK

    @pl.when(g == 0)
    def _():
        h = h_ref[...]
        u = h * _rms_scale(h) * mod_ref[0, 2:3, :] * (1.0 + mod_ref[0, 1:2, :]) + mod_ref[0, 0:1, :]
        u_ref[...] = u.astype(BF16)
        logits = _dot_hilo(u, wr_ref[...]) + br_ref[...]
        lane_i = lax.broadcasted_iota(jnp.int32, logits.shape, 1)
        lane = lane_i.astype(F32)
        lane_grp = (lane_i // EXPERTS_PER_GROUP).astype(F32)
        neg = -jnp.inf
        big = float(lanes)
        is_grp = (lane_i >= n_exp) & (lane_i < n_exp + n_grp)
        gl = jnp.where(is_grp, logits, neg)
        gmax = jnp.max(gl, axis=-1, keepdims=True)
        p_group = 1.0 / jnp.sum(jnp.where(is_grp, jnp.exp(gl - gmax), 0.0), axis=-1, keepdims=True)
        g_idx = jnp.min(jnp.where(gl == gmax, lane - n_exp, big), axis=-1, keepdims=True)
        in_grp = (lane_i < n_exp) & (lane_grp == g_idx)
        el = jnp.where(in_grp, logits, neg)
        m1 = jnp.max(el, axis=-1, keepdims=True)
        i1 = jnp.min(jnp.where(el == m1, lane, big), axis=-1, keepdims=True)
        el2 = jnp.where(lane == i1, neg, el)
        m2 = jnp.max(el2, axis=-1, keepdims=True)
        i2 = jnp.min(jnp.where(el2 == m2, lane, big), axis=-1, keepdims=True)
        e21 = jnp.exp(m2 - m1)
        p1 = 1.0 / (1.0 + e21)
        p2 = e21 * p1
        gate_ref[...] = p_group * (jnp.where(lane == i1, p1, 0.0) + jnp.where(lane == i2, p2, 0.0))
        member = jnp.where(lane == g_idx, 1.0, 0.0)
        tri = (lax.broadcasted_iota(jnp.int32, (tm, tm), 1) <= lax.broadcasted_iota(jnp.int32, (tm, tm), 0))
        count = jnp.dot(jnp.where(tri, 1.0, 0.0).astype(BF16), member.astype(BF16), preferred_element_type=F32)
        slot = jnp.where(member > 0.0, count - 1.0, -1.0)
        slotc_ref[...] = slot
        slott_ref[...] = slot.T
        o_ref[...] = jnp.zeros_like(o_ref)

    lane_t = lax.broadcasted_iota(jnp.int32, (tm, lanes), 1)
    slot_col = jnp.sum(jnp.where(lane_t == g, slotc_ref[...], 0.0), axis=-1, keepdims=True)
    slot_cb = jnp.broadcast_to(slot_col, (tm, sb))
    slot_row = slott_ref[pl.ds(g, 1), :]
    n_blk = ((jnp.max(slot_col, axis=0, keepdims=True) + float(sb)) * (1.0 / sb)).astype(jnp.int32)[0, 0]
    gates = gate_ref[...]
    gates_hi = gates.astype(BF16)
    gates_lo = (gates - gates_hi.astype(F32)).astype(BF16)
    s_iota = lax.broadcasted_iota(jnp.int32, (sb, tm), 0).astype(F32)
    l_iota = lax.broadcasted_iota(jnp.int32, (tm, sb), 1).astype(F32)
    lane_s = lax.broadcasted_iota(jnp.int32, (sb, lanes), 1)

    def block(blk, carry):
        base = (blk * sb).astype(F32)
        sel = jnp.where(slot_row == s_iota + base, 1.0, 0.0).astype(BF16)
        x = jnp.dot(sel, u_ref[...], preferred_element_type=F32).astype(BF16)
        gs = (jnp.dot(sel, gates_hi, preferred_element_type=F32)
              + jnp.dot(sel, gates_lo, preferred_element_type=F32))
        hs = []
        for e in range(EXPERTS_PER_GROUP):
            ge = jnp.sum(jnp.where(lane_s == g * EXPERTS_PER_GROUP + e, gs, 0.0), axis=-1, keepdims=True)
            hg = jnp.dot(x, wg_ref[e], preferred_element_type=F32)
            hu = jnp.dot(x, wu_ref[e], preferred_element_type=F32)
            hs.append((hg * _sigmoid(hg) * hu * ge).astype(BF16))
        wd = wd_ref[...]
        y = jnp.dot(jnp.concatenate(hs, axis=1), wd.reshape(wd.shape[0] * wd.shape[1], wd.shape[2]),
                    preferred_element_type=F32).astype(BF16)
        sel_t = jnp.where(slot_cb == l_iota + base, 1.0, 0.0).astype(BF16)
        o_ref[...] += jnp.dot(sel_t, y, preferred_element_type=F32)
        return carry

    lax.fori_loop(0, n_blk, block, 0)

    @pl.when(g == pl.num_programs(1) - 1)
    def _():
        f = o_ref[...]
        o_ref[...] = h_ref[...] + mod_ref[0, 3:4, :] * (f * _rms_scale(f) * mod_ref[0, 4:5, :])


def _moe(h1, mod, w_router, b_router, w_gate, w_up, w_down, tokens_per_batch):
    n, d = h1.shape
    n_exp, _, d_exp = w_gate.shape
    tm = min(1024, tokens_per_batch)
    exp_blk = EXPERTS_PER_GROUP
    assert n % tm == 0 and tokens_per_batch % tm == 0 and n_exp % exp_blk == 0 and tm % MOE_SLOT_BLOCK == 0
    per_b = tokens_per_batch // tm
    lanes = w_router.shape[1]
    return pl.pallas_call(
        functools.partial(_moe_kernel, n_exp=n_exp),
        out_shape=jax.ShapeDtypeStruct((n, d), F32),
        grid=(n // tm, n_exp // exp_blk),
        in_specs=[pl.BlockSpec((tm, d), lambda i, j: (i, 0)),
                  pl.BlockSpec((1, 8, d), lambda i, j: (i // per_b, 0, 0)),
                  pl.BlockSpec((d, lanes), lambda i, j: (0, 0)),
                  pl.BlockSpec((1, lanes), lambda i, j: (0, 0)),
                  pl.BlockSpec((exp_blk, d, d_exp), lambda i, j: (j, 0, 0)),
                  pl.BlockSpec((exp_blk, d, d_exp), lambda i, j: (j, 0, 0)),
                  pl.BlockSpec((exp_blk, d_exp, d), lambda i, j: (j, 0, 0))],
        out_specs=pl.BlockSpec((tm, d), lambda i, j: (i, 0)),
        scratch_shapes=[pltpu.VMEM((tm, d), BF16), pltpu.VMEM((tm, lanes), F32), pltpu.VMEM((tm, lanes), F32),
                        pltpu.VMEM((lanes, tm), F32)],
        compiler_params=pltpu.CompilerParams(dimension_semantics=("parallel", "arbitrary"),
                                             vmem_limit_bytes=VMEM_LIMIT),
        name="moe",
    )(h1, mod, w_router, b_router, w_gate, w_up, w_down)


def _pad_rows(a, rows):
    return jnp.concatenate([a, jnp.zeros((rows - a.shape[0],) + a.shape[1:], a.dtype)], axis=0)


def _layer(l, h_lat, h_ctx, c, c_ctx, ada_w, ada_b, norm_gains, w_in, conv_w, pool_w, pool_scale, pool_proj,
           w0, w2, a0, a2, g2, k_k, k_a, r_k, lnx_w, lnx_b, rwkv_proj, w_out, router_group_w, router_group_b,
           router_expert_w, router_expert_b, expert_w_gate, expert_w_up, expert_w_down):
    b, t, d = h_lat.shape
    n_ctx = h_ctx.shape[1]
    d_pool = pool_scale.shape[1]
    d_rwkv = k_k.shape[1]
    d_conv = conv_w.shape[2]
    n_exp = expert_w_gate.shape[1]

    cc = _pad_rows(jnp.concatenate([c, c_ctx[None, :]], axis=0), -(-(b + 1) // 8) * 8)
    ada = _ada(cc, ada_w[l], ada_b[l][None, :])
    sh1, sc1, gt1, sh2, sc2, gt2 = [ada[:b, i * d:(i + 1) * d] for i in range(6)]
    csh1 = jnp.broadcast_to(ada[b, 0:d], (b, d))
    csc1 = jnp.broadcast_to(ada[b, d:2 * d], (b, d))
    gains = [jnp.broadcast_to(norm_gains[l, i], (b, d)) for i in range(4)]
    zero = jnp.zeros((b, d), F32)
    mod1 = jnp.stack([sh1, sc1, csh1, csc1, gains[0], gt1, gains[1], zero], axis=1)
    mod2 = jnp.stack([sh2, sc2, gains[2], gt2, gains[3], zero, zero, zero], axis=1)

    w_in_b = w_in[l].astype(BF16)
    w_pool = w_in_b[:, :d_pool]
    w_conv = w_in_b[:, d_pool:d_pool + d_conv]
    w_gate = w_in_b[:, d_pool + d_conv:]

    col_pad = -d_conv % 512
    pconv, xg = _inproj_conv(h_ctx, h_lat, mod1, jnp.pad(w_conv, ((0, 0), (0, col_pad))),
                             jnp.pad(conv_w[l], ((0, 8 - conv_w.shape[1]), (0, col_pad))), d_conv - LORA_G)
    pool_in, gates = _inproj_pg(h_lat, mod1, w_pool, w_gate)

    zw = jnp.zeros_like(w2[l, 0])
    za = jnp.zeros_like(a2[l, 0])
    w2p = jnp.stack([jnp.concatenate([w2[l, 0], zw], axis=0), jnp.concatenate([zw, w2[l, 1]], axis=0)]).astype(BF16)
    a2p = jnp.stack([jnp.concatenate([a2[l, 0], za], axis=0), jnp.concatenate([za, a2[l, 1]], axis=0)]).astype(BF16)
    vecs = _pad_rows(jnp.stack([w0[l, 0], w0[l, 1], a0[l, 0], a0[l, 1], k_k[l], k_a[l], r_k[l].reshape(-1),
                                lnx_w[l], lnx_b[l]]), 16)
    yf, yb = _rwkv(pconv, w2p, a2p, vecs, n_ctx, d_rwkv)

    pool_y = _pool(pool_in, pool_w[l], pool_scale[l])
    h1 = _merge(h_lat, pool_y, yf, yb, xg, gates, mod1, g2[l].astype(BF16), pool_proj[l].astype(BF16),
                rwkv_proj[l].astype(BF16), w_out[l].astype(BF16))

    lanes = 128
    w_router = jnp.concatenate([router_expert_w[l], router_group_w[l],
                                jnp.zeros((d, lanes - n_exp - N_GROUPS), F32)], axis=1)
    b_router = jnp.concatenate([router_expert_b[l], router_group_b[l],
                                jnp.zeros((lanes - n_exp - N_GROUPS,), F32)])[None, :]
    h2 = _moe(h1.reshape(b * t, d), mod2, w_router, b_router, expert_w_gate[l].astype(BF16),
              expert_w_up[l].astype(BF16), expert_w_down[l].astype(BF16), t)
    return h2.reshape(b, t, d)


def kernel(x, c, ctx, c_ctx, ada_w, ada_b, norm_gains, w_in, conv_w, pool_w, pool_scale, pool_proj, w0, w2, a0, a2, g2, k_k, k_a, r_k, lnx_w, lnx_b, rwkv_proj, w_out, router_group_w, router_group_b, router_expert_w, router_expert_b, expert_w_gate, expert_w_up, expert_w_down):
    depth = ada_w.shape[0]
    assert depth == 1, "the context stream update between layers is not implemented"
    h_lat = x.astype(F32)
    h_ctx = ctx.astype(F32)
    h_lat = _layer(0, h_lat, h_ctx, c, c_ctx, ada_w, ada_b, norm_gains, w_in, conv_w, pool_w, pool_scale,
                   pool_proj, w0, w2, a0, a2, g2, k_k, k_a, r_k, lnx_w, lnx_b, rwkv_proj, w_out,
                   router_group_w, router_group_b, router_expert_w, router_expert_b,
                   expert_w_gate, expert_w_up, expert_w_down)
    return h_lat.astype(x.dtype)
```

```python
import functools
import math

import jax
import jax.numpy as jnp
from jax import lax
from jax.experimental import pallas as pl
from jax.experimental.pallas import tpu as pltpu

F32 = jnp.float32
BF16 = jnp.bfloat16

GRID_W = 64
POOL_WINDOWS = (2, 4, 8, 16)
POOL_GROUP_DIM = 128
HEAD_DIM = 64
LORA_W = 64
LORA_A = 64
LORA_G = 128
N_GROUPS = 4
EXPERTS_PER_GROUP = 8
GN_EPS = 64e-5
RMS_EPS = 1e-6
DECAY_OFFSET = 0.5

CHUNK = 64
HEADS_PER_LANE_GROUP = 4
LANE_GROUP = HEADS_PER_LANE_GROUP * HEAD_DIM
RWKV_BATCH_BLOCK = 2
MOE_SLOT_BLOCK = 128
VMEM_LIMIT = 56 * 1024 * 1024


def _bdot(a, b):
    return jnp.dot(a.astype(BF16), b.astype(BF16), preferred_element_type=F32)


def _bdot_t(a, b):
    return lax.dot_general(a.astype(BF16), b.astype(BF16), (((1,), (1,)), ((), ())),
                           preferred_element_type=F32)


def _split3(a):
    hi = a.astype(BF16)
    r1 = a - hi.astype(F32)
    mid = r1.astype(BF16)
    lo = (r1 - mid.astype(F32)).astype(BF16)
    return hi, mid, lo


def _dot_f32(a, b):
    a0, a1, a2 = _split3(a)
    b0, b1, b2 = _split3(b)
    d = lambda p, q: jnp.dot(p, q, preferred_element_type=F32)
    return (d(a0, b0) + (d(a0, b1) + d(a1, b0))
            + (d(a0, b2) + d(a1, b1) + d(a2, b0)))


def _dot_hilo(a, b):
    a0 = a.astype(BF16)
    a1 = (a - a0.astype(F32)).astype(BF16)
    b0 = b.astype(BF16)
    b1 = (b - b0.astype(F32)).astype(BF16)
    d = lambda p, q: jnp.dot(p, q, preferred_element_type=F32)
    return d(a0, b0) + (d(a0, b1) + d(a1, b0))


def _sigmoid(x):
    return 1.0 / (1.0 + jnp.exp(-x))


def _rms_scale(x):
    return lax.rsqrt(jnp.mean(x * x, axis=-1, keepdims=True) + RMS_EPS)


def _ada_kernel(c_ref, w_ref, b_ref, o_ref):
    c = c_ref[...]
    o_ref[...] = _dot_f32(c * _sigmoid(c), w_ref[...]) + b_ref[...]


def _ada(cc, ada_w, ada_b):
    rows, d = cc.shape
    n = ada_w.shape[1]
    tn = 1024
    return pl.pallas_call(
        _ada_kernel,
        out_shape=jax.ShapeDtypeStruct((rows, n), F32),
        grid=(n // tn,),
        in_specs=[pl.BlockSpec((rows, d), lambda j: (0, 0)),
                  pl.BlockSpec((d, tn), lambda j: (0, j)),
                  pl.BlockSpec((1, tn), lambda j: (0, j))],
        out_specs=pl.BlockSpec((rows, tn), lambda j: (0, j)),
        compiler_params=pltpu.CompilerParams(dimension_semantics=("arbitrary",),
                                             vmem_limit_bytes=VMEM_LIMIT),
        name="ada",
    )(cc, ada_w, ada_b)


def _inproj_conv_kernel(c_ref, x_ref, mod_ref, w_ref, cw_ref, o_ref, xg_ref, hn_ref, *, row_blk, lat_blk, xg_col):
    n_ctx = c_ref.shape[1]
    seq = n_ctx + x_ref.shape[1]

    @pl.when(pl.program_id(1) == 0)
    def _():
        gain = mod_ref[0, 4:5, :]
        for r0 in range(0, seq, row_blk):
            is_ctx = r0 < n_ctx
            sh = mod_ref[0, 2:3, :] if is_ctx else mod_ref[0, 0:1, :]
            sc = mod_ref[0, 3:4, :] if is_ctx else mod_ref[0, 1:2, :]
            x = c_ref[0, r0:r0 + row_blk, :] if is_ctx else x_ref[0, r0 - n_ctx:r0 - n_ctx + row_blk, :]
            hn = x * _rms_scale(x) * gain * (1.0 + sc) + sh
            hn_ref[r0:r0 + row_blk, :] = hn.astype(BF16)

    def conv(q, first_row, masked):
        n = q.shape[0]
        prev = pltpu.roll(q, 1, 0)
        nxt = pltpu.roll(q, n - 1, 0)
        if masked:
            row = lax.broadcasted_iota(jnp.int32, q.shape, 0) + first_row
            prev = jnp.where((row == 0) | (row == n_ctx), 0.0, prev)
            nxt = jnp.where((row == n_ctx - 1) | (row == seq - 1), 0.0, nxt)
        return (cw_ref[0:1, :] * prev + cw_ref[1:2, :] * q + cw_ref[2:3, :] * nxt).astype(BF16)

    halo = 16
    blocks = [(0, n_ctx)] + [(r, r + lat_blk) for r in range(n_ctx, seq, lat_blk)]
    for r0, r1 in blocks:
        at_start = r0 in (0, n_ctx)
        at_end = r1 in (n_ctx, seq)
        w0 = r0 if at_start else r0 - halo
        w1 = r1 if at_end else r1 + halo
        q = jnp.dot(hn_ref[w0:w1, :], w_ref[...], preferred_element_type=F32)
        o_ref[0, r0:r1, :] = conv(q, w0, False)[r0 - w0:r1 - w0]
        if at_start:
            o_ref[0, r0:r0 + halo, :] = conv(q[0:2 * halo], w0, True)[0:halo]
        if at_end:
            o_ref[0, r1 - halo:r1, :] = conv(q[w1 - w0 - 2 * halo:], w1 - 2 * halo, True)[halo:]

    tn = o_ref.shape[2]

    @pl.when(pl.program_id(1) == xg_col // tn)
    def _():
        xg_ref[0] = o_ref[0, n_ctx:, xg_col % tn:xg_col % tn + LORA_G]


def _inproj_conv(h_ctx, h_lat, mod, w_conv, conv_w, xg_col):
    b, t, d = h_lat.shape
    n_ctx = h_ctx.shape[1]
    seq = n_ctx + t
    n = w_conv.shape[1]
    tn = 512
    row_blk = math.gcd(math.gcd(t, n_ctx), 256)
    lat_blk = math.gcd(t, 512)
    assert n % tn == 0 and xg_col % LORA_G == 0 and min(lat_blk, n_ctx) >= 32 and n_ctx % 16 == 0
    return pl.pallas_call(
        functools.partial(_inproj_conv_kernel, row_blk=row_blk, lat_blk=lat_blk, xg_col=xg_col),
        out_shape=(jax.ShapeDtypeStruct((b, seq, n), BF16), jax.ShapeDtypeStruct((b, t, LORA_G), BF16)),
        grid=(b, n // tn),
        in_specs=[pl.BlockSpec((1, n_ctx, d), lambda i, j: (i, 0, 0)),
                  pl.BlockSpec((1, t, d), lambda i, j: (i, 0, 0)),
                  pl.BlockSpec((1, 8, d), lambda i, j: (i, 0, 0)),
                  pl.BlockSpec((d, tn), lambda i, j: (0, j)),
                  pl.BlockSpec((8, tn), lambda i, j: (0, j))],
        out_specs=(pl.BlockSpec((1, seq, tn), lambda i, j: (i, 0, j)),
                   pl.BlockSpec((1, t, LORA_G), lambda i, j: (i, 0, 0))),
        scratch_shapes=[pltpu.VMEM((seq, d), BF16)],
        compiler_params=pltpu.CompilerParams(dimension_semantics=("parallel", "arbitrary"),
                                             vmem_limit_bytes=VMEM_LIMIT),
        name="inproj_conv",
    )(h_ctx, h_lat, mod, w_conv, conv_w)


def _inproj_pg_kernel(x_ref, mod_ref, wp_ref, wg_ref, pool_ref, gate_ref):
    x = x_ref[0]
    hn = (x * _rms_scale(x) * mod_ref[0, 4:5, :] * (1.0 + mod_ref[0, 1:2, :]) + mod_ref[0, 0:1, :]).astype(BF16)
    pool_ref[0] = jnp.dot(hn, wp_ref[...], preferred_element_type=F32).astype(BF16)
    gate_ref[0] = _sigmoid(jnp.dot(hn, wg_ref[...], preferred_element_type=F32)).astype(BF16)


def _inproj_pg(x, mod, w_pool, w_gate):
    b, t, d = x.shape
    tm = min(512, t)
    assert t % tm == 0
    npool, ngate = w_pool.shape[1], w_gate.shape[1]
    return pl.pallas_call(
        _inproj_pg_kernel,
        out_shape=(jax.ShapeDtypeStruct((b, t, npool), BF16), jax.ShapeDtypeStruct((b, t, ngate), BF16)),
        grid=(b, t // tm),
        in_specs=[pl.BlockSpec((1, tm, d), lambda i, j: (i, j, 0)),
                  pl.BlockSpec((1, 8, d), lambda i, j: (i, 0, 0)),
                  pl.BlockSpec((d, npool), lambda i, j: (0, 0)),
                  pl.BlockSpec((d, ngate), lambda i, j: (0, 0))],
        out_specs=(pl.BlockSpec((1, tm, npool), lambda i, j: (i, j, 0)),
                   pl.BlockSpec((1, tm, ngate), lambda i, j: (i, j, 0))),
        compiler_params=pltpu.CompilerParams(dimension_semantics=("parallel", "parallel"),
                                             vmem_limit_bytes=VMEM_LIMIT),
        name="inproj_pool_gate",
    )(x, mod, w_pool, w_gate)


def _stack(x, bd):
    xb = x.astype(BF16)
    return jnp.concatenate([xb] * HEADS_PER_LANE_GROUP, axis=0) * bd


def _rwkv_kernel(xf_ref, xb_ref, w2_ref, a2_ref, vec_ref, bd_ref, icat_ref, mask_ref, tri_ref,
                 of_ref, ob_ref, s_ref, *, d_rwkv):
    step = pl.program_id(1)

    @pl.when(step == 0)
    def _():
        s_ref[...] = jnp.zeros_like(s_ref)

    bd = bd_ref[...]
    icat = icat_ref[...]
    n_lane_groups = d_rwkv // LANE_GROUP
    o_xw = 3 * d_rwkv
    o_xa = o_xw + 2 * LORA_W

    def bmm(a, b):
        return jnp.dot(a.astype(BF16), _stack(b, bd), preferred_element_type=F32)

    n_streams = 2 * xf_ref.shape[0]
    probs = [(d, g) for d in range(n_streams) for g in range(n_lane_groups)]
    x_refs = [(xf_ref, xb_ref)[d % 2].at[pl.ds(d // 2, 1)] for d in range(n_streams)]
    o_refs = [(of_ref, ob_ref)[d % 2].at[pl.ds(d // 2, 1)] for d in range(n_streams)]
    C = CHUNK
    icat_b = icat.astype(BF16)

    def lanes(g):
        return slice(g * LANE_GROUP, (g + 1) * LANE_GROUP)

    def each(fn, *lists):
        return [fn(*args) for args in zip(*lists)]

    def split_rows(x, n):
        return [x[i * C:(i + 1) * C] for i in range(n)]

    def shared_bd_dot(blocks):
        return split_rows(_bdot(jnp.concatenate(blocks, axis=0), bd), len(blocks))

    tanh_xw = [jnp.tanh(x_refs[d][0, :, o_xw:o_xw + 2 * LORA_W].astype(F32)).astype(BF16)
               for d in range(n_streams)]
    xa = [x_refs[d][0, :, o_xa:o_xa + 2 * LORA_A] for d in range(n_streams)]
    r = [x_refs[d][0, :, lanes(g)].astype(F32) for d, g in probs]
    k = [x_refs[d][0, :, d_rwkv + g * LANE_GROUP:d_rwkv + (g + 1) * LANE_GROUP].astype(F32) for d, g in probs]
    v = [x_refs[d][0, :, 2 * d_rwkv + g * LANE_GROUP:2 * d_rwkv + (g + 1) * LANE_GROUP].astype(F32)
         for d, g in probs]

    z = [vec_ref[d % 2:d % 2 + 1, lanes(g)]
         + jnp.dot(tanh_xw[d], w2_ref[d % 2, :, lanes(g)], preferred_element_type=F32) for d, g in probs]
    lw = [-math.exp(-DECAY_OFFSET) * _sigmoid(zi) for zi in z]
    a = [_sigmoid(vec_ref[2 + d % 2:3 + d % 2, lanes(g)]
                  + jnp.dot(xa[d], a2_ref[d % 2, :, lanes(g)], preferred_element_type=F32)) for d, g in probs]
    kkr = [ki * vec_ref[4:5, lanes(g)] for ki, (d, g) in zip(k, probs)]
    kd = [ki * (1.0 + (ai - 1.0) * vec_ref[5:6, lanes(g)]) for ki, ai, (d, g) in zip(k, a, probs)]
    rkr = [ri * kdi * vec_ref[6:7, lanes(g)] for ri, kdi, (d, g) in zip(r, kd, probs)]
    sums = shared_bd_dot([x * x for x in kkr] + rkr)
    kk = [x * lax.rsqrt(jnp.maximum(ss, 1e-24)) for x, ss in zip(kkr, sums[:len(probs)])]
    bonus = [bs * vi for bs, vi in zip(sums[len(probs):], v)]

    def cumdecay(lwi, d):
        l0 = lwi.astype(BF16)
        l1 = (lwi - l0.astype(F32)).astype(BF16)
        cs = jnp.dot(tri_ref[d % 2], jnp.concatenate([l0, l1], axis=1), preferred_element_type=F32)
        return cs[:, :LANE_GROUP] + cs[:, LANE_GROUP:]

    cl = [cumdecay(lwi, d) for lwi, (d, g) in zip(lw, probs)]
    e_pos = [jnp.exp(x) for x in cl]
    e_neg = [jnp.exp(-x) for x in cl]
    at = [-kki * jnp.exp(cli - lwi) for kki, cli, lwi in zip(kk, cl, lw)]
    rt = each(lambda ri, e: ri * e, r, e_pos)
    bt = each(lambda kki, ai, e: kki * ai * e, kk, a, e_neg)
    kt = each(lambda kdi, e: kdi * e, kd, e_neg)

    lhs = each(lambda ai, ri: jnp.concatenate([ai, ri, icat], axis=0).astype(BF16), at, rt)
    gb = each(lambda l, x: _bdot_t(l, _stack(x, bd)), lhs, bt)
    gk = each(lambda l, x: _bdot_t(l, _stack(x, bd)), lhs, kt)
    strict = [mask_ref[d % 2, 0] for d, g in probs]
    incl = [mask_ref[d % 2, 1] for d, g in probs]
    a_ab = each(lambda x, m: x[0:C] * m, gb, strict)
    a_rb = each(lambda x, m: x[C:2 * C] * m, gb, incl)
    bt_t = [x[2 * C:3 * C] for x in gb]
    a_ak = each(lambda x, m: x[0:C] * m, gk, strict)
    a_rk = each(lambda x, m: x[C:2 * C] * m, gk, incl)
    kt_t = [x[2 * C:3 * C] for x in gk]

    tinv = [icat + x for x in a_ab]
    lp = each(bmm, a_ab, a_ab)
    for _ in range(4):
        both = each(lambda l, t: jnp.dot(jnp.concatenate([l, t], axis=0).astype(BF16), _stack(l, bd),
                                         preferred_element_type=F32), lp, tinv)
        lp = [x[0:C] for x in both]
        tinv = each(lambda t, x: t + x[C:2 * C], tinv, both)
    tinv = each(lambda t, l: t + bmm(t, l), tinv, lp)

    hv = each(lambda p, q_, t_, vi: jnp.dot(jnp.concatenate([p, q_, t_], axis=0).astype(BF16), _stack(vi, bd),
                                            preferred_element_type=F32), a_ak, a_rk, kt_t, v)
    lhs2 = each(lambda p, q_, t: jnp.dot(jnp.concatenate([p, q_], axis=0).astype(BF16), _stack(t, bd),
                                         preferred_element_type=F32).astype(BF16), a_rb, bt_t, tinv)
    hw = each(lambda l, x: jnp.dot(l, _stack(x, bd), preferred_element_type=F32), lhs2, at)
    hu = each(lambda l, h: jnp.dot(l, _stack(h[0:C], bd), preferred_element_type=F32), lhs2, hv)

    def gam_rows(e, d):
        last = C - 1 if d % 2 == 0 else 0
        gam = e[last:last + 1, :]
        gam_hi = gam.astype(BF16)
        gam_lo = (gam - gam_hi.astype(F32)).astype(BF16)
        return [icat_b * gam_hi, icat_b * gam_lo]

    gparts = split_rows(jnp.dot(jnp.concatenate([blk for e, (d, g) in zip(e_pos, probs) for blk in gam_rows(e, d)],
                                                axis=0), bd, preferred_element_type=F32), 2 * len(probs))
    gcat = [gparts[2 * i] + gparts[2 * i + 1] for i in range(len(probs))]

    s0 = [s_ref[d, g] for d, g in probs]
    hs = each(lambda ri, hwi, s: jnp.dot(jnp.concatenate([ri + hwi[0:C], hwi[C:2 * C]], axis=0).astype(BF16),
                                         _stack(s, bd), preferred_element_type=F32), rt, hw, s0)
    y = each(lambda hsi, hui, hvi: hsi[0:C] + hui[0:C] + hvi[C:2 * C], hs, hu, hv)
    for (d, g), gc, s, hsi, hui, hvi in zip(probs, gcat, s0, hs, hu, hv):
        s_ref[d, g] = gc * (s + hsi[C:2 * C] + hui[C:2 * C] + hvi[2 * C:3 * C])

    inv_n = 1.0 / HEAD_DIM
    mu = [m * inv_n for m in shared_bd_dot(y)]
    yc = each(lambda yi, m: yi - m, y, mu)
    var = [m * inv_n for m in shared_bd_dot([x * x for x in yc])]
    for (d, g), yci, vari, bi in zip(probs, yc, var, bonus):
        out = yci * lax.rsqrt(vari + GN_EPS) * vec_ref[7:8, lanes(g)] + vec_ref[8:9, lanes(g)] + bi
        o_refs[d][0, :, lanes(g)] = out.astype(BF16)


def _rwkv(pconv, w2p, a2p, vecs, n_ctx, d_rwkv):
    b, seq, n = pconv.shape
    nc = seq // CHUNK
    nc_ctx = n_ctx // CHUNK
    assert seq % CHUNK == 0 and n_ctx % CHUNK == 0 and d_rwkv % LANE_GROUP == 0
    n_lane_groups = d_rwkv // LANE_GROUP

    lane = jnp.arange(LANE_GROUP)
    rowi = jnp.arange(CHUNK)
    j = (lane % HEAD_DIM)[None, :]
    t = rowi[:, None]
    bd = (lane[:, None] // HEAD_DIM == lane[None, :] // HEAD_DIM).astype(BF16)
    icat = (j == t).astype(F32)
    masks = jnp.stack([jnp.stack([(j < t), (j <= t)]), jnp.stack([(j > t), (j >= t)])]).astype(F32)
    tri = jnp.stack([rowi[None, :] <= rowi[:, None], rowi[None, :] >= rowi[:, None]]).astype(BF16)

    def bwd_chunk(s):
        return jnp.where(s < nc_ctx, nc_ctx - 1 - s, nc + nc_ctx - 1 - s)

    nc_lat = nc - nc_ctx
    out_f = lambda i, s: (i, jnp.where(s < nc_ctx, nc_lat, s - nc_ctx), 0)
    out_b = lambda i, s: (i, jnp.where(s < nc_ctx, nc_lat, nc - 1 - s), 0)
    out_sds = jax.ShapeDtypeStruct((b, seq - n_ctx + CHUNK, d_rwkv), BF16)

    nb = RWKV_BATCH_BLOCK if b % RWKV_BATCH_BLOCK == 0 else 1
    const2 = lambda i, s: (0, 0)
    const3 = lambda i, s: (0, 0, 0)
    const4 = lambda i, s: (0, 0, 0, 0)
    return pl.pallas_call(
        functools.partial(_rwkv_kernel, d_rwkv=d_rwkv),
        out_shape=(out_sds, out_sds),
        grid=(b // nb, nc),
        in_specs=[pl.BlockSpec((nb, CHUNK, n), lambda i, s: (i, s, 0)),
                  pl.BlockSpec((nb, CHUNK, n), lambda i, s: (i, bwd_chunk(s), 0)),
                  pl.BlockSpec(w2p.shape, const3),
                  pl.BlockSpec(a2p.shape, const3),
                  pl.BlockSpec(vecs.shape, const2),
                  pl.BlockSpec(bd.shape, const2),
                  pl.BlockSpec(icat.shape, const2),
                  pl.BlockSpec(masks.shape, const4),
                  pl.BlockSpec(tri.shape, const3)],
        out_specs=(pl.BlockSpec((nb, CHUNK, d_rwkv), out_f), pl.BlockSpec((nb, CHUNK, d_rwkv), out_b)),
        scratch_shapes=[pltpu.VMEM((2 * nb, n_lane_groups, CHUNK, LANE_GROUP), F32)],
        compiler_params=pltpu.CompilerParams(dimension_semantics=("parallel", "arbitrary"),
                                             vmem_limit_bytes=VMEM_LIMIT),
        name="rwkv_scan",
    )(pconv, pconv, w2p, a2p, vecs, bd, icat, masks, tri)


def _pool_kernel(u_ref, win_ref, inv_ref, w_ref, sc_ref, o_ref):
    nb = u_ref.shape[0]
    cg = u_ref.shape[2]
    u = jnp.concatenate([u_ref[i] for i in range(nb)], axis=1)
    total = jnp.dot(win_ref[0], u, preferred_element_type=F32)
    for i in range(nb):
        diff = total[:, i * cg:(i + 1) * cg] * inv_ref[0] - u_ref[i].astype(F32)
        o_ref[i] = (_bdot(diff, w_ref[0]) * sc_ref[0]).astype(BF16)


def _pool(pool_in, pool_w, pool_scale):
    b, t, dp = pool_in.shape
    ng = len(POOL_WINDOWS)
    cg = POOL_GROUP_DIM
    assert dp == ng * cg and t % GRID_W == 0
    tok = jnp.arange(t)
    row, col = tok // GRID_W, tok % GRID_W
    wins, invs = [], []
    for win in POOL_WINDOWS:
        lo = win // 2
        hi = win - lo - 1
        dr = row[None, :] - row[:, None]
        dc = col[None, :] - col[:, None]
        m = (dr >= -lo) & (dr <= hi) & (dc >= -lo) & (dc <= hi)
        wins.append(m.astype(BF16))
        invs.append(1.0 / jnp.sum(m, axis=1, dtype=F32))
    wins = jnp.stack(wins)
    invs = jnp.broadcast_to(jnp.stack(invs)[:, :, None], (ng, t, cg))
    nb = 2 if b % 2 == 0 else 1
    return pl.pallas_call(
        _pool_kernel,
        out_shape=jax.ShapeDtypeStruct((b, t, dp), BF16),
        grid=(ng, b // nb),
        in_specs=[pl.BlockSpec((nb, t, cg), lambda g, i: (i, 0, g)),
                  pl.BlockSpec((1, t, t), lambda g, i: (g, 0, 0)),
                  pl.BlockSpec((1, t, cg), lambda g, i: (g, 0, 0)),
                  pl.BlockSpec((1, cg, cg), lambda g, i: (g, 0, 0)),
                  pl.BlockSpec((1, 1, cg), lambda g, i: (g, 0, 0))],
        out_specs=pl.BlockSpec((nb, t, cg), lambda g, i: (i, 0, g)),
        compiler_params=pltpu.CompilerParams(dimension_semantics=("arbitrary", "arbitrary"),
                                             vmem_limit_bytes=VMEM_LIMIT),
        name="pool_mix",
    )(pool_in, wins, invs, pool_w.astype(BF16), pool_scale.reshape(ng, 1, cg))


def _merge_kernel(x_ref, py_ref, yf_ref, yb_ref, xg_ref, gate_ref, mod_ref, g2_ref, pp_ref, rp_ref, wo_ref, o_ref):
    d = x_ref.shape[2]
    g = jnp.dot(_sigmoid(xg_ref[0].astype(F32)).astype(BF16), g2_ref[...], preferred_element_type=F32)
    ry = (yf_ref[0].astype(F32) + yb_ref[0].astype(F32)) * g
    gates = gate_ref[0]
    m = (gates[:, :d].astype(F32) * jnp.dot(py_ref[0], pp_ref[...], preferred_element_type=F32)
         + gates[:, d:].astype(F32) * _bdot(ry, rp_ref[...]))
    mo = _bdot(m, wo_ref[...])
    o_ref[0] = x_ref[0] + mod_ref[0, 5:6, :] * (mo * _rms_scale(mo) * mod_ref[0, 6:7, :])


def _merge(x, pool_y, yf, yb, xg, gates, mod, g2, pool_proj, rwkv_proj, w_out):
    b, t, d = x.shape
    tm = math.gcd(t, 512)
    dr = yf.shape[2]
    lat = lambda i, j: (i, j, 0)
    const = lambda i, j: (0, 0)
    return pl.pallas_call(
        _merge_kernel,
        out_shape=jax.ShapeDtypeStruct((b, t, d), F32),
        grid=(b, t // tm),
        in_specs=[pl.BlockSpec((1, tm, d), lat),
                  pl.BlockSpec((1, tm, pool_y.shape[2]), lat),
                  pl.BlockSpec((1, tm, dr), lat),
                  pl.BlockSpec((1, tm, dr), lat),
                  pl.BlockSpec((1, tm, LORA_G), lat),
                  pl.BlockSpec((1, tm, gates.shape[2]), lat),
                  pl.BlockSpec((1, 8, d), lambda i, j: (i, 0, 0)),
                  pl.BlockSpec(g2.shape, const),
                  pl.BlockSpec(pool_proj.shape, const),
                  pl.BlockSpec(rwkv_proj.shape, const),
                  pl.BlockSpec(w_out.shape, const)],
        out_specs=pl.BlockSpec((1, tm, d), lat),
        compiler_params=pltpu.CompilerParams(dimension_semantics=("parallel", "parallel"),
                                             vmem_limit_bytes=VMEM_LIMIT),
        name="merge_out",
    )(x, pool_y, yf, yb, xg, gates, mod, g2, pool_proj, rwkv_proj, w_out)


def _moe_kernel(h_ref, mod_ref, wr_ref, br_ref, wg_ref, wu_ref, wd_ref, o_ref, u_ref, gate_ref, slotc_ref, slott_ref,
                *, n_exp):
    g = pl.program_id(1)
    tm = h_ref.shape[0]
    lanes = gate_ref.shape[1]
    n_grp = n_exp // EXPERTS_PER_GROUP
    sb = MOE_SLOT_BLOCK

    @pl.when(g == 0)
    def _():
        h = h_ref[...]
        u = h * _rms_scale(h) * mod_ref[0, 2:3, :] * (1.0 + mod_ref[0, 1:2, :]) + mod_ref[0, 0:1, :]
        u_ref[...] = u.astype(BF16)
        logits = _dot_hilo(u, wr_ref[...]) + br_ref[...]
        lane_i = lax.broadcasted_iota(jnp.int32, logits.shape, 1)
        lane = lane_i.astype(F32)
        lane_grp = (lane_i // EXPERTS_PER_GROUP).astype(F32)
        neg = -jnp.inf
        big = float(lanes)
        is_grp = (lane_i >= n_exp) & (lane_i < n_exp + n_grp)
        gl = jnp.where(is_grp, logits, neg)
        gmax = jnp.max(gl, axis=-1, keepdims=True)
        p_group = 1.0 / jnp.sum(jnp.where(is_grp, jnp.exp(gl - gmax), 0.0), axis=-1, keepdims=True)
        g_idx = jnp.min(jnp.where(gl == gmax, lane - n_exp, big), axis=-1, keepdims=True)
        in_grp = (lane_i < n_exp) & (lane_grp == g_idx)
        el = jnp.where(in_grp, logits, neg)
        m1 = jnp.max(el, axis=-1, keepdims=True)
        i1 = jnp.min(jnp.where(el == m1, lane, big), axis=-1, keepdims=True)
        el2 = jnp.where(lane == i1, neg, el)
        m2 = jnp.max(el2, axis=-1, keepdims=True)
        i2 = jnp.min(jnp.where(el2 == m2, lane, big), axis=-1, keepdims=True)
        e21 = jnp.exp(m2 - m1)
        p1 = 1.0 / (1.0 + e21)
        p2 = e21 * p1
        gate_ref[...] = p_group * (jnp.where(lane == i1, p1, 0.0) + jnp.where(lane == i2, p2, 0.0))
        member = jnp.where(lane == g_idx, 1.0, 0.0)
        tri = (lax.broadcasted_iota(jnp.int32, (tm, tm), 1) <= lax.broadcasted_iota(jnp.int32, (tm, tm), 0))
        count = jnp.dot(jnp.where(tri, 1.0, 0.0).astype(BF16), member.astype(BF16), preferred_element_type=F32)
        slot = jnp.where(member > 0.0, count - 1.0, -1.0)
        slotc_ref[...] = slot
        slott_ref[...] = slot.T
        o_ref[...] = jnp.zeros_like(o_ref)

    lane_t = lax.broadcasted_iota(jnp.int32, (tm, lanes), 1)
    slot_col = jnp.sum(jnp.where(lane_t == g, slotc_ref[...], 0.0), axis=-1, keepdims=True)
    slot_cb = jnp.broadcast_to(slot_col, (tm, sb))
    slot_row = slott_ref[pl.ds(g, 1), :]
    n_blk = ((jnp.max(slot_col, axis=0, keepdims=True) + float(sb)) * (1.0 / sb)).astype(jnp.int32)[0, 0]
    gates = gate_ref[...]
    gates_hi = gates.astype(BF16)
    gates_lo = (gates - gates_hi.astype(F32)).astype(BF16)
    s_iota = lax.broadcasted_iota(jnp.int32, (sb, tm), 0).astype(F32)
    l_iota = lax.broadcasted_iota(jnp.int32, (tm, sb), 1).astype(F32)
    lane_s = lax.broadcasted_iota(jnp.int32, (sb, lanes), 1)

    def block(blk, carry):
        base = (blk * sb).astype(F32)
        sel = jnp.where(slot_row == s_iota + base, 1.0, 0.0).astype(BF16)
        x = jnp.dot(sel, u_ref[...], preferred_element_type=F32).astype(BF16)
        gs = (jnp.dot(sel, gates_hi, preferred_element_type=F32)
              + jnp.dot(sel, gates_lo, preferred_element_type=F32))
        hs = []
        for e in range(EXPERTS_PER_GROUP):
            ge = jnp.sum(jnp.where(lane_s == g * EXPERTS_PER_GROUP + e, gs, 0.0), axis=-1, keepdims=True)
            hg = jnp.dot(x, wg_ref[e], preferred_element_type=F32)
            hu = jnp.dot(x, wu_ref[e], preferred_element_type=F32)
            hs.append((hg * _sigmoid(hg) * hu * ge).astype(BF16))
        wd = wd_ref[...]
        y = jnp.dot(jnp.concatenate(hs, axis=1), wd.reshape(wd.shape[0] * wd.shape[1], wd.shape[2]),
                    preferred_element_type=F32).astype(BF16)
        sel_t = jnp.where(slot_cb == l_iota + base, 1.0, 0.0).astype(BF16)
        o_ref[...] += jnp.dot(sel_t, y, preferred_element_type=F32)
        return carry

    lax.fori_loop(0, n_blk, block, 0)

    @pl.when(g == pl.num_programs(1) - 1)
    def _():
        f = o_ref[...]
        o_ref[...] = h_ref[...] + mod_ref[0, 3:4, :] * (f * _rms_scale(f) * mod_ref[0, 4:5, :])


def _moe(h1, mod, w_router, b_router, w_gate, w_up, w_down, tokens_per_batch):
    n, d = h1.shape
    n_exp, _, d_exp = w_gate.shape
    tm = min(1024, tokens_per_batch)
    exp_blk = EXPERTS_PER_GROUP
    assert n % tm == 0 and tokens_per_batch % tm == 0 and n_exp % exp_blk == 0 and tm % MOE_SLOT_BLOCK == 0
    per_b = tokens_per_batch // tm
    lanes = w_router.shape[1]
    return pl.pallas_call(
        functools.partial(_moe_kernel, n_exp=n_exp),
        out_shape=jax.ShapeDtypeStruct((n, d), F32),
        grid=(n // tm, n_exp // exp_blk),
        in_specs=[pl.BlockSpec((tm, d), lambda i, j: (i, 0)),
                  pl.BlockSpec((1, 8, d), lambda i, j: (i // per_b, 0, 0)),
                  pl.BlockSpec((d, lanes), lambda i, j: (0, 0)),
                  pl.BlockSpec((1, lanes), lambda i, j: (0, 0)),
                  pl.BlockSpec((exp_blk, d, d_exp), lambda i, j: (j, 0, 0)),
                  pl.BlockSpec((exp_blk, d, d_exp), lambda i, j: (j, 0, 0)),
                  pl.BlockSpec((exp_blk, d_exp, d), lambda i, j: (j, 0, 0))],
        out_specs=pl.BlockSpec((tm, d), lambda i, j: (i, 0)),
        scratch_shapes=[pltpu.VMEM((tm, d), BF16), pltpu.VMEM((tm, lanes), F32), pltpu.VMEM((tm, lanes), F32),
                        pltpu.VMEM((lanes, tm), F32)],
        compiler_params=pltpu.CompilerParams(dimension_semantics=("parallel", "arbitrary"),
                                             vmem_limit_bytes=VMEM_LIMIT),
        name="moe",
    )(h1, mod, w_router, b_router, w_gate, w_up, w_down)


def _pad_rows(a, rows):
    return jnp.concatenate([a, jnp.zeros((rows - a.shape[0],) + a.shape[1:], a.dtype)], axis=0)


def _layer(l, h_lat, h_ctx, c, c_ctx, ada_w, ada_b, norm_gains, w_in, conv_w, pool_w, pool_scale, pool_proj,
           w0, w2, a0, a2, g2, k_k, k_a, r_k, lnx_w, lnx_b, rwkv_proj, w_out, router_group_w, router_group_b,
           router_expert_w, router_expert_b, expert_w_gate, expert_w_up, expert_w_down):
    b, t, d = h_lat.shape
    n_ctx = h_ctx.shape[1]
    d_pool = pool_scale.shape[1]
    d_rwkv = k_k.shape[1]
    d_conv = conv_w.shape[2]
    n_exp = expert_w_gate.shape[1]

    cc = _pad_rows(jnp.concatenate([c, c_ctx[None, :]], axis=0), -(-(b + 1) // 8) * 8)
    ada = _ada(cc, ada_w[l], ada_b[l][None, :])
    sh1, sc1, gt1, sh2, sc2, gt2 = [ada[:b, i * d:(i + 1) * d] for i in range(6)]
    csh1 = jnp.broadcast_to(ada[b, 0:d], (b, d))
    csc1 = jnp.broadcast_to(ada[b, d:2 * d], (b, d))
    gains = [jnp.broadcast_to(norm_gains[l, i], (b, d)) for i in range(4)]
    zero = jnp.zeros((b, d), F32)
    mod1 = jnp.stack([sh1, sc1, csh1, csc1, gains[0], gt1, gains[1], zero], axis=1)
    mod2 = jnp.stack([sh2, sc2, gains[2], gt2, gains[3], zero, zero, zero], axis=1)

    w_in_b = w_in[l].astype(BF16)
    w_pool = w_in_b[:, :d_pool]
    w_conv = w_in_b[:, d_pool:d_pool + d_conv]
    w_gate = w_in_b[:, d_pool + d_conv:]

    col_pad = -d_conv % 512
    pconv, xg = _inproj_conv(h_ctx, h_lat, mod1, jnp.pad(w_conv, ((0, 0), (0, col_pad))),
                             jnp.pad(conv_w[l], ((0, 8 - conv_w.shape[1]), (0, col_pad))), d_conv - LORA_G)
    pool_in, gates = _inproj_pg(h_lat, mod1, w_pool, w_gate)

    zw = jnp.zeros_like(w2[l, 0])
    za = jnp.zeros_like(a2[l, 0])
    w2p = jnp.stack([jnp.concatenate([w2[l, 0], zw], axis=0), jnp.concatenate([zw, w2[l, 1]], axis=0)]).astype(BF16)
    a2p = jnp.stack([jnp.concatenate([a2[l, 0], za], axis=0), jnp.concatenate([za, a2[l, 1]], axis=0)]).astype(BF16)
    vecs = _pad_rows(jnp.stack([w0[l, 0], w0[l, 1], a0[l, 0], a0[l, 1], k_k[l], k_a[l], r_k[l].reshape(-1),
                                lnx_w[l], lnx_b[l]]), 16)
    yf, yb = _rwkv(pconv, w2p, a2p, vecs, n_ctx, d_rwkv)

    pool_y = _pool(pool_in, pool_w[l], pool_scale[l])
    h1 = _merge(h_lat, pool_y, yf, yb, xg, gates, mod1, g2[l].astype(BF16), pool_proj[l].astype(BF16),
                rwkv_proj[l].astype(BF16), w_out[l].astype(BF16))

    lanes = 128
    w_router = jnp.concatenate([router_expert_w[l], router_group_w[l],
                                jnp.zeros((d, lanes - n_exp - N_GROUPS), F32)], axis=1)
    b_router = jnp.concatenate([router_expert_b[l], router_group_b[l],
                                jnp.zeros((lanes - n_exp - N_GROUPS,), F32)])[None, :]
    h2 = _moe(h1.reshape(b * t, d), mod2, w_router, b_router, expert_w_gate[l].astype(BF16),
              expert_w_up[l].astype(BF16), expert_w_down[l].astype(BF16), t)
    return h2.reshape(b, t, d)


def kernel(x, c, ctx, c_ctx, ada_w, ada_b, norm_gains, w_in, conv_w, pool_w, pool_scale, pool_proj, w0, w2, a0, a2, g2, k_k, k_a, r_k, lnx_w, lnx_b, rwkv_proj, w_out, router_group_w, router_group_b, router_expert_w, router_expert_b, expert_w_gate, expert_w_up, expert_w_down):
    depth = ada_w.shape[0]
    assert depth == 1, "the context stream update between layers is not implemented"
    h_lat = x.astype(F32)
    h_ctx = ctx.astype(F32)
    h_lat = _layer(0, h_lat, h_ctx, c, c_ctx, ada_w, ada_b, norm_gains, w_in, conv_w, pool_w, pool_scale,
                   pool_proj, w0, w2, a0, a2, g2, k_k, k_a, r_k, lnx_w, lnx_b, rwkv_proj, w_out,
                   router_group_w, router_group_b, router_expert_w, router_expert_b,
                   expert_w_gate, expert_w_up, expert_w_down)
    return h_lat.astype(x.dtype)
```

```python
import functools
import math

import jax
import jax.numpy as jnp
from jax import lax
from jax.experimental import pallas as pl
from jax.experimental.pallas import tpu as pltpu

F32 = jnp.float32
BF16 = jnp.bfloat16

GRID_W = 64
POOL_WINDOWS = (2, 4, 8, 16)
POOL_GROUP_DIM = 128
HEAD_DIM = 64
LORA_W = 64
LORA_A = 64
LORA_G = 128
N_GROUPS = 4
EXPERTS_PER_GROUP = 8
GN_EPS = 64e-5
RMS_EPS = 1e-6
DECAY_OFFSET = 0.5

CHUNK = 64
HEADS_PER_LANE_GROUP = 4
LANE_GROUP = HEADS_PER_LANE_GROUP * HEAD_DIM
RWKV_BATCH_BLOCK = 4
MOE_SLOT_BLOCK = 128
VMEM_LIMIT = 56 * 1024 * 1024


def _bdot(a, b):
    return jnp.dot(a.astype(BF16), b.astype(BF16), preferred_element_type=F32)


def _bdot_t(a, b):
    return lax.dot_general(a.astype(BF16), b.astype(BF16), (((1,), (1,)), ((), ())),
                           preferred_element_type=F32)


def _split3(a):
    hi = a.astype(BF16)
    r1 = a - hi.astype(F32)
    mid = r1.astype(BF16)
    lo = (r1 - mid.astype(F32)).astype(BF16)
    return hi, mid, lo


def _dot_f32(a, b):
    a0, a1, a2 = _split3(a)
    b0, b1, b2 = _split3(b)
    d = lambda p, q: jnp.dot(p, q, preferred_element_type=F32)
    return (d(a0, b0) + (d(a0, b1) + d(a1, b0))
            + (d(a0, b2) + d(a1, b1) + d(a2, b0)))


def _dot_hilo(a, b):
    a0 = a.astype(BF16)
    a1 = (a - a0.astype(F32)).astype(BF16)
    b0 = b.astype(BF16)
    b1 = (b - b0.astype(F32)).astype(BF16)
    d = lambda p, q: jnp.dot(p, q, preferred_element_type=F32)
    return d(a0, b0) + (d(a0, b1) + d(a1, b0))


def _sigmoid(x):
    return 1.0 / (1.0 + jnp.exp(-x))


def _rms_scale(x):
    return lax.rsqrt(jnp.mean(x * x, axis=-1, keepdims=True) + RMS_EPS)


def _ada_kernel(c_ref, w_ref, b_ref, o_ref):
    c = c_ref[...]
    o_ref[...] = _dot_f32(c * _sigmoid(c), w_ref[...]) + b_ref[...]


def _ada(cc, ada_w, ada_b):
    rows, d = cc.shape
    n = ada_w.shape[1]
    tn = 1024
    return pl.pallas_call(
        _ada_kernel,
        out_shape=jax.ShapeDtypeStruct((rows, n), F32),
        grid=(n // tn,),
        in_specs=[pl.BlockSpec((rows, d), lambda j: (0, 0)),
                  pl.BlockSpec((d, tn), lambda j: (0, j)),
                  pl.BlockSpec((1, tn), lambda j: (0, j))],
        out_specs=pl.BlockSpec((rows, tn), lambda j: (0, j)),
        compiler_params=pltpu.CompilerParams(dimension_semantics=("arbitrary",),
                                             vmem_limit_bytes=VMEM_LIMIT),
        name="ada",
    )(cc, ada_w, ada_b)


def _inproj_conv_kernel(c_ref, x_ref, mod_ref, w_ref, cw_ref, o_ref, xg_ref, hn_ref, *, row_blk, lat_blk, xg_col):
    n_ctx = c_ref.shape[1]
    seq = n_ctx + x_ref.shape[1]

    @pl.when(pl.program_id(1) == 0)
    def _():
        gain = mod_ref[0, 4:5, :]
        for r0 in range(0, seq, row_blk):
            is_ctx = r0 < n_ctx
            sh = mod_ref[0, 2:3, :] if is_ctx else mod_ref[0, 0:1, :]
            sc = mod_ref[0, 3:4, :] if is_ctx else mod_ref[0, 1:2, :]
            x = c_ref[0, r0:r0 + row_blk, :] if is_ctx else x_ref[0, r0 - n_ctx:r0 - n_ctx + row_blk, :]
            hn = x * _rms_scale(x) * gain * (1.0 + sc) + sh
            hn_ref[r0:r0 + row_blk, :] = hn.astype(BF16)

    def conv(q, first_row, masked):
        n = q.shape[0]
        prev = pltpu.roll(q, 1, 0)
        nxt = pltpu.roll(q, n - 1, 0)
        if masked:
            row = lax.broadcasted_iota(jnp.int32, q.shape, 0) + first_row
            prev = jnp.where((row == 0) | (row == n_ctx), 0.0, prev)
            nxt = jnp.where((row == n_ctx - 1) | (row == seq - 1), 0.0, nxt)
        return (cw_ref[0:1, :] * prev + cw_ref[1:2, :] * q + cw_ref[2:3, :] * nxt).astype(BF16)

    halo = 16
    blocks = [(0, n_ctx)] + [(r, r + lat_blk) for r in range(n_ctx, seq, lat_blk)]
    for r0, r1 in blocks:
        at_start = r0 in (0, n_ctx)
        at_end = r1 in (n_ctx, seq)
        w0 = r0 if at_start else r0 - halo
        w1 = r1 if at_end else r1 + halo
        q = jnp.dot(hn_ref[w0:w1, :], w_ref[...], preferred_element_type=F32)
        o_ref[0, r0:r1, :] = conv(q, w0, False)[r0 - w0:r1 - w0]
        if at_start:
            o_ref[0, r0:r0 + halo, :] = conv(q[0:2 * halo], w0, True)[0:halo]
        if at_end:
            o_ref[0, r1 - halo:r1, :] = conv(q[w1 - w0 - 2 * halo:], w1 - 2 * halo, True)[halo:]

    tn = o_ref.shape[2]

    @pl.when(pl.program_id(1) == xg_col // tn)
    def _():
        xg_ref[0] = o_ref[0, n_ctx:, xg_col % tn:xg_col % tn + LORA_G]


def _inproj_conv(h_ctx, h_lat, mod, w_conv, conv_w, xg_col):
    b, t, d = h_lat.shape
    n_ctx = h_ctx.shape[1]
    seq = n_ctx + t
    n = w_conv.shape[1]
    tn = 512
    row_blk = math.gcd(math.gcd(t, n_ctx), 256)
    lat_blk = math.gcd(t, 512)
    assert n % tn == 0 and xg_col % LORA_G == 0 and min(lat_blk, n_ctx) >= 32 and n_ctx % 16 == 0
    return pl.pallas_call(
        functools.partial(_inproj_conv_kernel, row_blk=row_blk, lat_blk=lat_blk, xg_col=xg_col),
        out_shape=(jax.ShapeDtypeStruct((b, seq, n), BF16), jax.ShapeDtypeStruct((b, t, LORA_G), BF16)),
        grid=(b, n // tn),
        in_specs=[pl.BlockSpec((1, n_ctx, d), lambda i, j: (i, 0, 0)),
                  pl.BlockSpec((1, t, d), lambda i, j: (i, 0, 0)),
                  pl.BlockSpec((1, 8, d), lambda i, j: (i, 0, 0)),
                  pl.BlockSpec((d, tn), lambda i, j: (0, j)),
                  pl.BlockSpec((8, tn), lambda i, j: (0, j))],
        out_specs=(pl.BlockSpec((1, seq, tn), lambda i, j: (i, 0, j)),
                   pl.BlockSpec((1, t, LORA_G), lambda i, j: (i, 0, 0))),
        scratch_shapes=[pltpu.VMEM((seq, d), BF16)],
        compiler_params=pltpu.CompilerParams(dimension_semantics=("parallel", "arbitrary"),
                                             vmem_limit_bytes=VMEM_LIMIT),
        name="inproj_conv",
    )(h_ctx, h_lat, mod, w_conv, conv_w)


def _inproj_pg_kernel(x_ref, mod_ref, wp_ref, wg_ref, pool_ref, gate_ref):
    x = x_ref[0]
    hn = (x * _rms_scale(x) * mod_ref[0, 4:5, :] * (1.0 + mod_ref[0, 1:2, :]) + mod_ref[0, 0:1, :]).astype(BF16)
    pool_ref[0] = jnp.dot(hn, wp_ref[...], preferred_element_type=F32).astype(BF16)
    gate_ref[0] = _sigmoid(jnp.dot(hn, wg_ref[...], preferred_element_type=F32)).astype(BF16)


def _inproj_pg(x, mod, w_pool, w_gate):
    b, t, d = x.shape
    tm = min(512, t)
    assert t % tm == 0
    npool, ngate = w_pool.shape[1], w_gate.shape[1]
    return pl.pallas_call(
        _inproj_pg_kernel,
        out_shape=(jax.ShapeDtypeStruct((b, t, npool), BF16), jax.ShapeDtypeStruct((b, t, ngate), BF16)),
        grid=(b, t // tm),
        in_specs=[pl.BlockSpec((1, tm, d), lambda i, j: (i, j, 0)),
                  pl.BlockSpec((1, 8, d), lambda i, j: (i, 0, 0)),
                  pl.BlockSpec((d, npool), lambda i, j: (0, 0)),
                  pl.BlockSpec((d, ngate), lambda i, j: (0, 0))],
        out_specs=(pl.BlockSpec((1, tm, npool), lambda i, j: (i, j, 0)),
                   pl.BlockSpec((1, tm, ngate), lambda i, j: (i, j, 0))),
        compiler_params=pltpu.CompilerParams(dimension_semantics=("parallel", "parallel"),
                                             vmem_limit_bytes=VMEM_LIMIT),
        name="inproj_pool_gate",
    )(x, mod, w_pool, w_gate)


def _stack(x, bd):
    xb = x.astype(BF16)
    return jnp.concatenate([xb] * HEADS_PER_LANE_GROUP, axis=0) * bd


def _rwkv_kernel(*refs, d_rwkv, emit_y):
    xf_ref, xb_ref, w2_ref, a2_ref, vec_ref, bd_ref, icat_ref, mask_ref, tri_ref = refs[:9]
    if emit_y:
        sinit_ref, of_ref, ob_ref, s_ref = refs[9:]
    else:
        sfin_ref, s_ref = refs[9:]
    step = pl.program_id(1)

    @pl.when(step == 0)
    def _():
        s_ref[...] = sinit_ref[0] if emit_y else jnp.zeros_like(s_ref)

    bd = bd_ref[...]
    icat = icat_ref[...]
    n_lane_groups = d_rwkv // LANE_GROUP
    o_xw = 3 * d_rwkv
    o_xa = o_xw + 2 * LORA_W

    def bmm(a, b):
        return jnp.dot(a.astype(BF16), _stack(b, bd), preferred_element_type=F32)

    n_streams = 2 * xf_ref.shape[0]
    probs = [(d, g) for d in range(n_streams) for g in range(n_lane_groups)]
    x_refs = [(xf_ref, xb_ref)[d % 2].at[pl.ds(d // 2, 1)] for d in range(n_streams)]
    C = CHUNK
    icat_b = icat.astype(BF16)

    def lanes(g):
        return slice(g * LANE_GROUP, (g + 1) * LANE_GROUP)

    def each(fn, *lists):
        return [fn(*args) for args in zip(*lists)]

    def split_rows(x, n):
        return [x[i * C:(i + 1) * C] for i in range(n)]

    def shared_bd_dot(blocks):
        return split_rows(_bdot(jnp.concatenate(blocks, axis=0), bd), len(blocks))

    tanh_xw = [jnp.tanh(x_refs[d][0, :, o_xw:o_xw + 2 * LORA_W].astype(F32)).astype(BF16)
               for d in range(n_streams)]
    xa = [x_refs[d][0, :, o_xa:o_xa + 2 * LORA_A] for d in range(n_streams)]
    r = [x_refs[d][0, :, lanes(g)].astype(F32) for d, g in probs]
    k = [x_refs[d][0, :, d_rwkv + g * LANE_GROUP:d_rwkv + (g + 1) * LANE_GROUP].astype(F32) for d, g in probs]
    v = [x_refs[d][0, :, 2 * d_rwkv + g * LANE_GROUP:2 * d_rwkv + (g + 1) * LANE_GROUP].astype(F32)
         for d, g in probs]

    z = [vec_ref[d % 2:d % 2 + 1, lanes(g)]
         + jnp.dot(tanh_xw[d], w2_ref[d % 2, :, lanes(g)], preferred_element_type=F32) for d, g in probs]
    lw = [-math.exp(-DECAY_OFFSET) * _sigmoid(zi) for zi in z]
    a = [_sigmoid(vec_ref[2 + d % 2:3 + d % 2, lanes(g)]
                  + jnp.dot(xa[d], a2_ref[d % 2, :, lanes(g)], preferred_element_type=F32)) for d, g in probs]
    kkr = [ki * vec_ref[4:5, lanes(g)] for ki, (d, g) in zip(k, probs)]
    kd = [ki * (1.0 + (ai - 1.0) * vec_ref[5:6, lanes(g)]) for ki, ai, (d, g) in zip(k, a, probs)]
    rkr = [ri * kdi * vec_ref[6:7, lanes(g)] for ri, kdi, (d, g) in zip(r, kd, probs)] if emit_y else []
    sums = shared_bd_dot([x * x for x in kkr] + rkr)
    kk = [x * lax.rsqrt(jnp.maximum(ss, 1e-24)) for x, ss in zip(kkr, sums[:len(probs)])]
    bonus = [bs * vi for bs, vi in zip(sums[len(probs):], v)]

    def cumdecay(lwi, d):
        l0 = lwi.astype(BF16)
        l1 = (lwi - l0.astype(F32)).astype(BF16)
        cs = jnp.dot(tri_ref[d % 2], jnp.concatenate([l0, l1], axis=1), preferred_element_type=F32)
        return cs[:, :LANE_GROUP] + cs[:, LANE_GROUP:]

    cl = [cumdecay(lwi, d) for lwi, (d, g) in zip(lw, probs)]
    e_pos = [jnp.exp(x) for x in cl]
    e_neg = [jnp.exp(-x) for x in cl]
    at = [-kki * jnp.exp(cli - lwi) for kki, cli, lwi in zip(kk, cl, lw)]
    bt = each(lambda kki, ai, e: kki * ai * e, kk, a, e_neg)
    kt = each(lambda kdi, e: kdi * e, kd, e_neg)

    if emit_y:
        rt = each(lambda ri, e: ri * e, r, e_pos)
        lhs = each(lambda ai, ri: jnp.concatenate([ai, ri, icat], axis=0).astype(BF16), at, rt)
    else:
        lhs = [jnp.concatenate([ai, icat], axis=0).astype(BF16) for ai in at]
    gb = each(lambda l, x: _bdot_t(l, _stack(x, bd)), lhs, bt)
    gk = each(lambda l, x: _bdot_t(l, _stack(x, bd)), lhs, kt)
    strict = [mask_ref[d % 2, 0] for d, g in probs]
    a_ab = each(lambda x, m: x[0:C] * m, gb, strict)
    bt_t = [x[-C:] for x in gb]
    a_ak = each(lambda x, m: x[0:C] * m, gk, strict)
    kt_t = [x[-C:] for x in gk]
    if emit_y:
        incl = [mask_ref[d % 2, 1] for d, g in probs]
        a_rb = each(lambda x, m: x[C:2 * C] * m, gb, incl)
        a_rk = each(lambda x, m: x[C:2 * C] * m, gk, incl)

    tinv = [icat + x for x in a_ab]
    lp = each(bmm, a_ab, a_ab)
    for _ in range(4):
        both = each(lambda l, t: jnp.dot(jnp.concatenate([l, t], axis=0).astype(BF16), _stack(l, bd),
                                         preferred_element_type=F32), lp, tinv)
        lp = [x[0:C] for x in both]
        tinv = each(lambda t, x: t + x[C:2 * C], tinv, both)
    tinv = each(lambda t, l: t + bmm(t, l), tinv, lp)

    hv_lhs = each(lambda p, q_, t_: [p, q_, t_], a_ak, a_rk, kt_t) if emit_y else each(lambda p, t_: [p, t_], a_ak, kt_t)
    hv = each(lambda blocks, vi: jnp.dot(jnp.concatenate(blocks, axis=0).astype(BF16), _stack(vi, bd),
                                         preferred_element_type=F32), hv_lhs, v)
    lhs2_in = each(lambda p, q_: jnp.concatenate([p, q_], axis=0), a_rb, bt_t) if emit_y else bt_t
    lhs2 = each(lambda x, t: jnp.dot(x.astype(BF16), _stack(t, bd), preferred_element_type=F32).astype(BF16),
                lhs2_in, tinv)
    hw = each(lambda l, x: jnp.dot(l, _stack(x, bd), preferred_element_type=F32), lhs2, at)
    hu = each(lambda l, h: jnp.dot(l, _stack(h[0:C], bd), preferred_element_type=F32), lhs2, hv)

    def gam_rows(e, d):
        last = C - 1 if d % 2 == 0 else 0
        gam = e[last:last + 1, :]
        gam_hi = gam.astype(BF16)
        gam_lo = (gam - gam_hi.astype(F32)).astype(BF16)
        return [icat_b * gam_hi, icat_b * gam_lo]

    gparts = split_rows(jnp.dot(jnp.concatenate([blk for e, (d, g) in zip(e_pos, probs) for blk in gam_rows(e, d)],
                                                axis=0), bd, preferred_element_type=F32), 2 * len(probs))
    gcat = [gparts[2 * i] + gparts[2 * i + 1] for i in range(len(probs))]

    s0 = [s_ref[d, g] for d, g in probs]
    if emit_y:
        hs_lhs = each(lambda ri, hwi: jnp.concatenate([ri + hwi[0:C], hwi[C:2 * C]], axis=0), rt, hw)
    else:
        hs_lhs = hw
    hs = each(lambda x, s: jnp.dot(x.astype(BF16), _stack(s, bd), preferred_element_type=F32), hs_lhs, s0)
    for (d, g), gc, s, hsi, hui, hvi in zip(probs, gcat, s0, hs, hu, hv):
        s_ref[d, g] = gc * (s + hsi[-C:] + hui[-C:] + hvi[-C:])

    if not emit_y:
        @pl.when(step == pl.num_programs(1) - 1)
        def _():
            sfin_ref[0] = s_ref[...]
        return

    o_refs =[(of_ref, ob_ref)[d % 2].at[pl.ds(d // 2, 1)] for d in range(n_streams)]
    y = each(lambda hsi, hui, hvi: hsi[0:C] + hui[0:C] + hvi[C:2 * C], hs, hu, hv)
    inv_n = 1.0 / HEAD_DIM
    mu = [m * inv_n for m in shared_bd_dot(y)]
    yc = each(lambda yi, m: yi - m, y, mu)
    var = [m * inv_n for m in shared_bd_dot([x * x for x in yc])]
    for (d, g), yci, vari, bi in zip(probs, yc, var, bonus):
        out = yci * lax.rsqrt(vari + GN_EPS) * vec_ref[7:8, lanes(g)] + vec_ref[8:9, lanes(g)] + bi
        o_refs[d][0, :, lanes(g)] = out.astype(BF16)


def _rwkv(pconv, w2p, a2p, vecs, n_ctx, d_rwkv):
    b, seq, n = pconv.shape
    nc = seq // CHUNK
    nc_ctx = n_ctx // CHUNK
    assert seq % CHUNK == 0 and n_ctx % CHUNK == 0 and d_rwkv % LANE_GROUP == 0
    n_lane_groups = d_rwkv // LANE_GROUP

    lane = jnp.arange(LANE_GROUP)
    rowi = jnp.arange(CHUNK)
    j = (lane % HEAD_DIM)[None, :]
    t = rowi[:, None]
    bd = (lane[:, None] // HEAD_DIM == lane[None, :] // HEAD_DIM).astype(BF16)
    icat = (j == t).astype(F32)
    masks = jnp.stack([jnp.stack([(j < t), (j <= t)]), jnp.stack([(j > t), (j >= t)])]).astype(F32)
    tri = jnp.stack([rowi[None, :] <= rowi[:, None], rowi[None, :] >= rowi[:, None]]).astype(BF16)

    nc_lat = nc - nc_ctx
    nb = RWKV_BATCH_BLOCK if b % RWKV_BATCH_BLOCK == 0 else 1
    const2 = lambda i, s: (0, 0)
    const3 = lambda i, s: (0, 0, 0)
    const4 = lambda i, s: (0, 0, 0, 0)
    state_shape = (b // nb, 2 * nb, n_lane_groups, CHUNK, LANE_GROUP)
    state_spec = pl.BlockSpec((1,) + state_shape[1:], lambda i, s: (i, 0, 0, 0, 0))
    params = pltpu.CompilerParams(dimension_semantics=("parallel", "arbitrary"), vmem_limit_bytes=VMEM_LIMIT)
    scratch = [pltpu.VMEM(state_shape[1:], F32)]

    def specs(fwd_chunk, bwd_chunk):
        return [pl.BlockSpec((nb, CHUNK, n), lambda i, s: (i, fwd_chunk(s), 0)),
                pl.BlockSpec((nb, CHUNK, n), lambda i, s: (i, bwd_chunk(s), 0)),
                pl.BlockSpec(w2p.shape, const3),
                pl.BlockSpec(a2p.shape, const3),
                pl.BlockSpec(vecs.shape, const2),
                pl.BlockSpec(bd.shape, const2),
                pl.BlockSpec(icat.shape, const2),
                pl.BlockSpec(masks.shape, const4),
                pl.BlockSpec(tri.shape, const3)]

    consts = (w2p, a2p, vecs, bd, icat, masks, tri)
    state = pl.pallas_call(
        functools.partial(_rwkv_kernel, d_rwkv=d_rwkv, emit_y=False),
        out_shape=jax.ShapeDtypeStruct(state_shape, F32),
        grid=(b // nb, nc_ctx),
        in_specs=specs(lambda s: s, lambda s: nc_ctx - 1 - s),
        out_specs=state_spec,
        scratch_shapes=scratch,
        compiler_params=params,
        name="rwkv_scan_ctx",
    )(pconv, pconv, *consts)
    out_sds = jax.ShapeDtypeStruct((b, seq - n_ctx, d_rwkv), BF16)
    return pl.pallas_call(
        functools.partial(_rwkv_kernel, d_rwkv=d_rwkv, emit_y=True),
        out_shape=(out_sds, out_sds),
        grid=(b // nb, nc_lat),
        in_specs=specs(lambda s: nc_ctx + s, lambda s: nc - 1 - s) + [state_spec],
        out_specs=(pl.BlockSpec((nb, CHUNK, d_rwkv), lambda i, s: (i, s, 0)),
                   pl.BlockSpec((nb, CHUNK, d_rwkv), lambda i, s: (i, nc_lat - 1 - s, 0))),
        scratch_shapes=scratch,
        compiler_params=params,
        name="rwkv_scan",
    )(pconv, pconv, *consts, state)


def _pool_kernel(u_ref, win_ref, inv_ref, w_ref, sc_ref, o_ref):
    nb = u_ref.shape[0]
    cg = u_ref.shape[2]
    u = jnp.concatenate([u_ref[i] for i in range(nb)], axis=1)
    total = jnp.dot(win_ref[0], u, preferred_element_type=F32)
    for i in range(nb):
        diff = total[:, i * cg:(i + 1) * cg] * inv_ref[0] - u_ref[i].astype(F32)
        o_ref[i] = (_bdot(diff, w_ref[0]) * sc_ref[0]).astype(BF16)


def _pool(pool_in, pool_w, pool_scale):
    b, t, dp = pool_in.shape
    ng = len(POOL_WINDOWS)
    cg = POOL_GROUP_DIM
    assert dp == ng * cg and t % GRID_W == 0
    tok = jnp.arange(t)
    row, col = tok // GRID_W, tok % GRID_W
    wins, invs = [], []
    for win in POOL_WINDOWS:
        lo = win // 2
        hi = win - lo - 1
        dr = row[None, :] - row[:, None]
        dc = col[None, :] - col[:, None]
        m = (dr >= -lo) & (dr <= hi) & (dc >= -lo) & (dc <= hi)
        wins.append(m.astype(BF16))
        invs.append(1.0 / jnp.sum(m, axis=1, dtype=F32))
    wins = jnp.stack(wins)
    invs = jnp.broadcast_to(jnp.stack(invs)[:, :, None], (ng, t, cg))
    nb = 2 if b % 2 == 0 else 1
    return pl.pallas_call(
        _pool_kernel,
        out_shape=jax.ShapeDtypeStruct((b, t, dp), BF16),
        grid=(ng, b // nb),
        in_specs=[pl.BlockSpec((nb, t, cg), lambda g, i: (i, 0, g)),
                  pl.BlockSpec((1, t, t), lambda g, i: (g, 0, 0)),
                  pl.BlockSpec((1, t, cg), lambda g, i: (g, 0, 0)),
                  pl.BlockSpec((1, cg, cg), lambda g, i: (g, 0, 0)),
                  pl.BlockSpec((1, 1, cg), lambda g, i: (g, 0, 0))],
        out_specs=pl.BlockSpec((nb, t, cg), lambda g, i: (i, 0, g)),
        compiler_params=pltpu.CompilerParams(dimension_semantics=("arbitrary", "arbitrary"),
                                             vmem_limit_bytes=VMEM_LIMIT),
        name="pool_mix",
    )(pool_in, wins, invs, pool_w.astype(BF16), pool_scale.reshape(ng, 1, cg))


def _merge_kernel(x_ref, py_ref, yf_ref, yb_ref, xg_ref, gate_ref, mod_ref, g2_ref, pp_ref, rp_ref, wo_ref, o_ref):
    d = x_ref.shape[2]
    g = jnp.dot(_sigmoid(xg_ref[0].astype(F32)).astype(BF16), g2_ref[...], preferred_element_type=F32)
    ry = (yf_ref[0].astype(F32) + yb_ref[0].astype(F32)) * g
    gates = gate_ref[0]
    m = (gates[:, :d].astype(F32) * jnp.dot(py_ref[0], pp_ref[...], preferred_element_type=F32)
         + gates[:, d:].astype(F32) * _bdot(ry, rp_ref[...]))
    mo = _bdot(m, wo_ref[...])
    o_ref[0] = x_ref[0] + mod_ref[0, 5:6, :] * (mo * _rms_scale(mo) * mod_ref[0, 6:7, :])


def _merge(x, pool_y, yf, yb, xg, gates, mod, g2, pool_proj, rwkv_proj, w_out):
    b, t, d = x.shape
    tm = math.gcd(t, 512)
    dr = yf.shape[2]
    lat = lambda i, j: (i, j, 0)
    const = lambda i, j: (0, 0)
    return pl.pallas_call(
        _merge_kernel,
        out_shape=jax.ShapeDtypeStruct((b, t, d), F32),
        grid=(b, t // tm),
        in_specs=[pl.BlockSpec((1, tm, d), lat),
                  pl.BlockSpec((1, tm, pool_y.shape[2]), lat),
                  pl.BlockSpec((1, tm, dr), lat),
                  pl.BlockSpec((1, tm, dr), lat),
                  pl.BlockSpec((1, tm, LORA_G), lat),
                  pl.BlockSpec((1, tm, gates.shape[2]), lat),
                  pl.BlockSpec((1, 8, d), lambda i, j: (i, 0, 0)),
                  pl.BlockSpec(g2.shape, const),
                  pl.BlockSpec(pool_proj.shape, const),
                  pl.BlockSpec(rwkv_proj.shape, const),
                  pl.BlockSpec(w_out.shape, const)],
        out_specs=pl.BlockSpec((1, tm, d), lat),
        compiler_params=pltpu.CompilerParams(dimension_semantics=("parallel", "parallel"),
                                             vmem_limit_bytes=VMEM_LIMIT),
        name="merge_out",
    )(x, pool_y, yf, yb, xg, gates, mod, g2, pool_proj, rwkv_proj, w_out)


def _moe_kernel(h_ref, mod_ref, wr_ref, br_ref, wg_ref, wu_ref, wd_ref, o_ref, u_ref, gate_ref, slotc_ref, slott_ref,
                *, n_exp):
    g = pl.program_id(1)
    tm = h_ref.shape[0]
    lanes = gate_ref.shape[1]
    n_grp = n_exp // EXPERTS_PER_GROUP
    sb = MOE_SLOT_BLOCK

    @pl.when(g == 0)
    def _():
        h = h_ref[...]
        u = h * _rms_scale(h) * mod_ref[0, 2:3, :] * (1.0 + mod_ref[0, 1:2, :]) + mod_ref[0, 0:1, :]
        u_ref[...] = u.astype(BF16)
        logits = _dot_hilo(u, wr_ref[...]) + br_ref[...]
        lane_i = lax.broadcasted_iota(jnp.int32, logits.shape, 1)
        lane = lane_i.astype(F32)
        lane_grp = (lane_i // EXPERTS_PER_GROUP).astype(F32)
        neg = -jnp.inf
        big = float(lanes)
        is_grp = (lane_i >= n_exp) & (lane_i < n_exp + n_grp)
        gl = jnp.where(is_grp, logits, neg)
        gmax = jnp.max(gl, axis=-1, keepdims=True)
        p_group = 1.0 / jnp.sum(jnp.where(is_grp, jnp.exp(gl - gmax), 0.0), axis=-1, keepdims=True)
        g_idx = jnp.min(jnp.where(gl == gmax, lane - n_exp, big), axis=-1, keepdims=True)
        in_grp = (lane_i < n_exp) & (lane_grp == g_idx)
        el = jnp.where(in_grp, logits, neg)
        m1 = jnp.max(el, axis=-1, keepdims=True)
        i1 = jnp.min(jnp.where(el == m1, lane, big), axis=-1, keepdims=True)
        el2 = jnp.where(lane == i1, neg, el)
        m2 = jnp.max(el2, axis=-1, keepdims=True)
        i2 = jnp.min(jnp.where(el2 == m2, lane, big), axis=-1, keepdims=True)
        e21 = jnp.exp(m2 - m1)
        p1 = 1.0 / (1.0 + e21)
        p2 = e21 * p1
        gate_ref[...] = p_group * (jnp.where(lane == i1, p1, 0.0) + jnp.where(lane == i2, p2, 0.0))
        member = jnp.where(lane == g_idx, 1.0, 0.0)
        tri = (lax.broadcasted_iota(jnp.int32, (tm, tm), 1) <= lax.broadcasted_iota(jnp.int32, (tm, tm), 0))
        count = jnp.dot(jnp.where(tri, 1.0, 0.0).astype(BF16), member.astype(BF16), preferred_element_type=F32)
        slot = jnp.where(member > 0.0, count - 1.0, -1.0)
        slotc_ref[...] = slot
        slott_ref[...] = slot.T
        o_ref[...] = jnp.zeros_like(o_ref)

    lane_t = lax.broadcasted_iota(jnp.int32, (tm, lanes), 1)
    slot_col = jnp.sum(jnp.where(lane_t == g, slotc_ref[...], 0.0), axis=-1, keepdims=True)
    slot_cb = jnp.broadcast_to(slot_col, (tm, sb))
    slot_row = slott_ref[pl.ds(g, 1), :]
    n_tok = jnp.max(slot_col, axis=0, keepdims=True).astype(jnp.int32)[0, 0] + 1
    n_blk = (n_tok + (sb - 1)) // sb
    gates = gate_ref[...]
    gates_hi = gates.astype(BF16)
    gates_lo = (gates - gates_hi.astype(F32)).astype(BF16)
    s_iota = lax.broadcasted_iota(jnp.int32, (sb, tm), 0).astype(F32)
    l_iota = lax.broadcasted_iota(jnp.int32, (tm, sb), 1).astype(F32)
    lane_s = lax.broadcasted_iota(jnp.int32, (sb, lanes), 1)

    def block(blk, carry):
        base = (blk * sb).astype(F32)
        sel = jnp.where(slot_row == s_iota + base, 1.0, 0.0).astype(BF16)
        x = jnp.dot(sel, u_ref[...], preferred_element_type=F32).astype(BF16)
        gs = (jnp.dot(sel, gates_hi, preferred_element_type=F32)
              + jnp.dot(sel, gates_lo, preferred_element_type=F32))
        hs = []
        for e in range(EXPERTS_PER_GROUP):
            ge = jnp.sum(jnp.where(lane_s == g * EXPERTS_PER_GROUP + e, gs, 0.0), axis=-1, keepdims=True)
            hg = jnp.dot(x, wg_ref[e], preferred_element_type=F32)
            hu = jnp.dot(x, wu_ref[e], preferred_element_type=F32)
            hs.append((hg * _sigmoid(hg) * hu * ge).astype(BF16))
        wd = wd_ref[...]
        y = jnp.dot(jnp.concatenate(hs, axis=1), wd.reshape(wd.shape[0] * wd.shape[1], wd.shape[2]),
                    preferred_element_type=F32).astype(BF16)
        sel_t = jnp.where(slot_cb == l_iota + base, 1.0, 0.0).astype(BF16)
        o_ref[...] += jnp.dot(sel_t, y, preferred_element_type=F32)
        return carry

    lax.fori_loop(0, n_blk, block, 0)

    @pl.when(g == pl.num_programs(1) - 1)
    def _():
        f = o_ref[...]
        o_ref[...] = h_ref[...] + mod_ref[0, 3:4, :] * (f * _rms_scale(f) * mod_ref[0, 4:5, :])


def _moe(h1, mod, w_router, b_router, w_gate, w_up, w_down, tokens_per_batch):
    n, d = h1.shape
    n_exp, _, d_exp = w_gate.shape
    tm = min(1024, tokens_per_batch)
    exp_blk = EXPERTS_PER_GROUP
    assert n % tm == 0 and tokens_per_batch % tm == 0 and n_exp % exp_blk == 0 and tm % MOE_SLOT_BLOCK == 0
    per_b = tokens_per_batch // tm
    lanes = w_router.shape[1]
    return pl.pallas_call(
        functools.partial(_moe_kernel, n_exp=n_exp),
        out_shape=jax.ShapeDtypeStruct((n, d), F32),
        grid=(n // tm, n_exp // exp_blk),
        in_specs=[pl.BlockSpec((tm, d), lambda i, j: (i, 0)),
                  pl.BlockSpec((1, 8, d), lambda i, j: (i // per_b, 0, 0)),
                  pl.BlockSpec((d, lanes), lambda i, j: (0, 0)),
                  pl.BlockSpec((1, lanes), lambda i, j: (0, 0)),
                  pl.BlockSpec((exp_blk, d, d_exp), lambda i, j: (j, 0, 0)),
                  pl.BlockSpec((exp_blk, d, d_exp), lambda i, j: (j, 0, 0)),
                  pl.BlockSpec((exp_blk, d_exp, d), lambda i, j: (j, 0, 0))],
        out_specs=pl.BlockSpec((tm, d), lambda i, j: (i, 0)),
        scratch_shapes=[pltpu.VMEM((tm, d), BF16), pltpu.VMEM((tm, lanes), F32), pltpu.VMEM((tm, lanes), F32),
                        pltpu.VMEM((lanes, tm), F32)],
        compiler_params=pltpu.CompilerParams(dimension_semantics=("parallel", "arbitrary"),
                                             vmem_limit_bytes=VMEM_LIMIT),
        name="moe",
    )(h1, mod, w_router, b_router, w_gate, w_up, w_down)


def _pad_rows(a, rows):
    return jnp.concatenate([a, jnp.zeros((rows - a.shape[0],) + a.shape[1:], a.dtype)], axis=0)


def _layer(l, h_lat, h_ctx, c, c_ctx, ada_w, ada_b, norm_gains, w_in, conv_w, pool_w, pool_scale, pool_proj,
           w0, w2, a0, a2, g2, k_k, k_a, r_k, lnx_w, lnx_b, rwkv_proj, w_out, router_group_w, router_group_b,
           router_expert_w, router_expert_b, expert_w_gate, expert_w_up, expert_w_down):
    b, t, d = h_lat.shape
    n_ctx = h_ctx.shape[1]
    d_pool = pool_scale.shape[1]
    d_rwkv = k_k.shape[1]
    d_conv = conv_w.shape[2]
    n_exp = expert_w_gate.shape[1]

    cc = _pad_rows(jnp.concatenate([c, c_ctx[None, :]], axis=0), -(-(b + 1) // 8) * 8)
    ada = _ada(cc, ada_w[l], ada_b[l][None, :])
    sh1, sc1, gt1, sh2, sc2, gt2 = [ada[:b, i * d:(i + 1) * d] for i in range(6)]
    csh1 = jnp.broadcast_to(ada[b, 0:d], (b, d))
    csc1 = jnp.broadcast_to(ada[b, d:2 * d], (b, d))
    gains = [jnp.broadcast_to(norm_gains[l, i], (b, d)) for i in range(4)]
    zero = jnp.zeros((b, d), F32)
    mod1 = jnp.stack([sh1, sc1, csh1, csc1, gains[0], gt1, gains[1], zero], axis=1)
    mod2 = jnp.stack([sh2, sc2, gains[2], gt2, gains[3], zero, zero, zero], axis=1)

    w_in_b = w_in[l].astype(BF16)
    w_pool = w_in_b[:, :d_pool]
    w_conv = w_in_b[:, d_pool:d_pool + d_conv]
    w_gate = w_in_b[:, d_pool + d_conv:]

    col_pad = -d_conv % 512
    pconv, xg = _inproj_conv(h_ctx, h_lat, mod1, jnp.pad(w_conv, ((0, 0), (0, col_pad))),
                             jnp.pad(conv_w[l], ((0, 8 - conv_w.shape[1]), (0, col_pad))), d_conv - LORA_G)
    pool_in, gates = _inproj_pg(h_lat, mod1, w_pool, w_gate)

    zw = jnp.zeros_like(w2[l, 0])
    za = jnp.zeros_like(a2[l, 0])
    w2p = jnp.stack([jnp.concatenate([w2[l, 0], zw], axis=0), jnp.concatenate([zw, w2[l, 1]], axis=0)]).astype(BF16)
    a2p = jnp.stack([jnp.concatenate([a2[l, 0], za], axis=0), jnp.concatenate([za, a2[l, 1]], axis=0)]).astype(BF16)
    vecs = _pad_rows(jnp.stack([w0[l, 0], w0[l, 1], a0[l, 0], a0[l, 1], k_k[l], k_a[l], r_k[l].reshape(-1),
                                lnx_w[l], lnx_b[l]]), 16)
    yf, yb = _rwkv(pconv, w2p, a2p, vecs, n_ctx, d_rwkv)

    pool_y = _pool(pool_in, pool_w[l], pool_scale[l])
    h1 = _merge(h_lat, pool_y, yf, yb, xg, gates, mod1, g2[l].astype(BF16), pool_proj[l].astype(BF16),
                rwkv_proj[l].astype(BF16), w_out[l].astype(BF16))

    lanes = 128
    w_router = jnp.concatenate([router_expert_w[l], router_group_w[l],
                                jnp.zeros((d, lanes - n_exp - N_GROUPS), F32)], axis=1)
    b_router = jnp.concatenate([router_expert_b[l], router_group_b[l],
                                jnp.zeros((lanes - n_exp - N_GROUPS,), F32)])[None, :]
    h2 = _moe(h1.reshape(b * t, d), mod2, w_router, b_router, expert_w_gate[l].astype(BF16),
              expert_w_up[l].astype(BF16), expert_w_down[l].astype(BF16), t)
    return h2.reshape(b, t, d)


def kernel(x, c, ctx, c_ctx, ada_w, ada_b, norm_gains, w_in, conv_w, pool_w, pool_scale, pool_proj, w0, w2, a0, a2, g2, k_k, k_a, r_k, lnx_w, lnx_b, rwkv_proj, w_out, router_group_w, router_group_b, router_expert_w, router_expert_b, expert_w_gate, expert_w_up, expert_w_down):
    depth = ada_w.shape[0]
    assert depth == 1, "the context stream update between layers is not implemented"
    h_lat = x.astype(F32)
    h_ctx = ctx.astype(F32)
    h_lat = _layer(0, h_lat, h_ctx, c, c_ctx, ada_w, ada_b, norm_gains, w_in, conv_w, pool_w, pool_scale,
                   pool_proj, w0, w2, a0, a2, g2, k_k, k_a, r_k, lnx_w, lnx_b, rwkv_proj, w_out,
                   router_group_w, router_group_b, router_expert_w, router_expert_b,
                   expert_w_gate, expert_w_up, expert_w_down)
    return h_lat.astype(x.dtype)
```
